```python
import jax, jax.numpy as jnp
from jax import lax
import numpy as np

D_MODEL = 1024
BATCH = 4
SEQ = 4096
DEPTH = 4

MEM_LEN = 256
HEAD_DIM = 64
N_SB_HEADS = 8
N_FOX_HEADS = 8
N_MEM_HEADS = 4
MEM_HEAD_DIM = 128
SB_W = N_SB_HEADS * HEAD_DIM
FOX_W = N_FOX_HEADS * HEAD_DIM
MEM_W = N_MEM_HEADS * MEM_HEAD_DIM
N_BRANCH = 3
IN_W = 3 * SB_W + 3 * FOX_W + N_FOX_HEADS + MEM_W
D_FF = ((8 * D_MODEL // 3 + 127) // 128) * 128
Q_BLOCK = 128
RMS_EPS = 1e-6

kernel_name = 'hybrid_sb_fox_mem_macaron'


def _rmsnorm(t, g):
    t32 = t.astype(jnp.float32)
    t32 = t32 * lax.rsqrt(jnp.mean(t32 * t32, axis=-1, keepdims=True) + RMS_EPS)
    return t32.astype(t.dtype) * g


def _swiglu(t, w_gate, w_up, w_down):
    return (jax.nn.silu(t @ w_gate) * (t @ w_up)) @ w_down


def _split_heads(t, n_heads):
    b, s, _ = t.shape
    return t.reshape(b, s, n_heads, -1).transpose(0, 2, 1, 3)


def _merge_heads(t):
    b, h, s, d = t.shape
    return t.transpose(0, 2, 1, 3).reshape(b, s, h * d)


def _query_blocks(t):
    b, h, s = t.shape[:3]
    t = t.reshape((b, h, s // Q_BLOCK, Q_BLOCK) + t.shape[3:])
    return jnp.moveaxis(t, 2, 0)


def _unblock(o):
    nb, b, h, blk, d = o.shape
    return jnp.moveaxis(o, 0, 2).reshape(b, h, nb * blk, d)


def _stick_breaking_attention(q, k, v):
    b, h, s_len, d = q.shape
    scale = d ** -0.5
    key_pos = jnp.arange(s_len)

    def block(args):
        qb, i = args
        z = jnp.einsum('bhqd,bhkd->bhqk', qb, k).astype(jnp.float32) * scale
        q_pos = i * Q_BLOCK + jnp.arange(Q_BLOCK)
        mask = key_pos[None, :] < q_pos[:, None]
        log_beta = jax.nn.log_sigmoid(z)
        log_not = jnp.where(mask, log_beta - z, 0.0)
        log_between = lax.cumsum(log_not, axis=3, reverse=True) - log_not
        w = jnp.where(mask, jnp.exp(log_beta + log_between), 0.0)
        return jnp.einsum('bhqk,bhkd->bhqd', w.astype(v.dtype), v)

    out = lax.map(block, (_query_blocks(q), jnp.arange(s_len // Q_BLOCK)))
    return _unblock(out)


def _forgetting_attention(q, k, v, log_f):
    b, h, s_len, d = q.shape
    scale = d ** -0.5
    key_pos = jnp.arange(s_len)
    c = lax.cumsum(log_f.astype(jnp.float32), axis=2)
    neg = jnp.finfo(jnp.float32).min

    def block(args):
        qb, cb, i = args
        z = jnp.einsum('bhqd,bhkd->bhqk', qb, k).astype(jnp.float32) * scale
        z = z + cb[..., :, None] - c[..., None, :]
        q_pos = i * Q_BLOCK + jnp.arange(Q_BLOCK)
        mask = key_pos[None, :] <= q_pos[:, None]
        p = jax.nn.softmax(jnp.where(mask, z, neg), axis=-1)
        return jnp.einsum('bhqk,bhkd->bhqd', p.astype(v.dtype), v)

    out = lax.map(block, (_query_blocks(q), _query_blocks(c), jnp.arange(s_len // Q_BLOCK)))
    return _unblock(out)


def _memory_attention(q, k, v):
    z = jnp.einsum('bhqd,bhkd->bhqk', q, k).astype(jnp.float32) * (q.shape[-1] ** -0.5)
    p = jax.nn.softmax(z, axis=-1)
    return jnp.einsum('bhqk,bhkd->bhqd', p.astype(v.dtype), v)


def setup_inputs(seed: int = 0) -> dict:
    key = jax.random.key(seed)
    ks = jax.random.split(key, 32)
    L = DEPTH

    def w(k, shape, fan_in):
        return jax.random.normal(k, shape, jnp.float32) * (fan_in ** -0.5)

    def gain(k, shape):
        return 1.0 + 0.05 * jax.random.normal(k, shape, jnp.float32)

    return {
        'x': jax.random.normal(ks[0], (BATCH, SEQ, D_MODEL), jnp.float32),
        'mem': jax.random.normal(ks[1], (BATCH, MEM_LEN, D_MODEL), jnp.float32),
        'ffn1_pre_g': gain(ks[2], (L, D_MODEL)),
        'ffn1_post_g': gain(ks[3], (L, D_MODEL)),
        'ffn1_w_gate': w(ks[4], (L, D_MODEL, D_FF), D_MODEL),
        'ffn1_w_up': w(ks[5], (L, D_MODEL, D_FF), D_MODEL),
        'ffn1_w_down': w(ks[6], (L, D_FF, D_MODEL), D_FF),
        'mix_pre_g': gain(ks[7], (L, D_MODEL)),
        'mix_post_g': gain(ks[8], (L, D_MODEL)),
        'w_in': w(ks[9], (L, D_MODEL, IN_W), D_MODEL),
        'b_forget': 2.0 + 0.5 * jax.random.normal(ks[10], (L, N_FOX_HEADS), jnp.float32),
        'mem_norm_g': gain(ks[11], (D_MODEL,)),
        'w_mem_kv': w(ks[12], (L, D_MODEL, 2 * MEM_W), D_MODEL),
        'w_gate': w(ks[13], (L, D_MODEL, N_BRANCH * D_MODEL), D_MODEL),
        'b_gate': 0.02 * jax.random.normal(ks[14], (L, N_BRANCH * D_MODEL), jnp.float32),
        'w_br_sb': w(ks[15], (L, SB_W, D_MODEL), SB_W),
        'w_br_fox': w(ks[16], (L, FOX_W, D_MODEL), FOX_W),
        'w_br_mem': w(ks[17], (L, MEM_W, D_MODEL), MEM_W),
        'w_out': w(ks[18], (L, D_MODEL, D_MODEL), D_MODEL),
        'ffn2_pre_g': gain(ks[19], (L, D_MODEL)),
        'ffn2_post_g': gain(ks[20], (L, D_MODEL)),
        'ffn2_w_gate': w(ks[21], (L, D_MODEL, D_FF), D_MODEL),
        'ffn2_w_up': w(ks[22], (L, D_MODEL, D_FF), D_MODEL),
        'ffn2_w_down': w(ks[23], (L, D_FF, D_MODEL), D_FF),
    }


def reference(x, mem, ffn1_pre_g, ffn1_post_g, ffn1_w_gate, ffn1_w_up, ffn1_w_down,
              mix_pre_g, mix_post_g, w_in, b_forget, mem_norm_g, w_mem_kv, w_gate, b_gate,
              w_br_sb, w_br_fox, w_br_mem, w_out,
              ffn2_pre_g, ffn2_post_g, ffn2_w_gate, ffn2_w_up, ffn2_w_down):
    mem_n = _rmsnorm(mem, mem_norm_g)
    split_at = np.cumsum([SB_W, SB_W, SB_W, FOX_W, FOX_W, FOX_W, N_FOX_HEADS])
    h = x
    for l in range(DEPTH):
        f = _swiglu(_rmsnorm(h, ffn1_pre_g[l]), ffn1_w_gate[l], ffn1_w_up[l], ffn1_w_down[l])
        h = h + 0.5 * _rmsnorm(f, ffn1_post_g[l])

        u = _rmsnorm(h, mix_pre_g[l])
        proj = u @ w_in[l]
        q_sb, k_sb, v_sb, q_fx, k_fx, v_fx, f_logit, q_mem = jnp.split(proj, split_at, axis=-1)

        o_sb = _stick_breaking_attention(_split_heads(q_sb, N_SB_HEADS), _split_heads(k_sb, N_SB_HEADS),
                                         _split_heads(v_sb, N_SB_HEADS))
        log_f = jax.nn.log_sigmoid((f_logit + b_forget[l]).astype(jnp.float32)).transpose(0, 2, 1)
        o_fx = _forgetting_attention(_split_heads(q_fx, N_FOX_HEADS), _split_heads(k_fx, N_FOX_HEADS),
                                     _split_heads(v_fx, N_FOX_HEADS), log_f)
        k_mem, v_mem = jnp.split(mem_n @ w_mem_kv[l], 2, axis=-1)
        o_mem = _memory_attention(_split_heads(q_mem, N_MEM_HEADS), _split_heads(k_mem, N_MEM_HEADS),
                                  _split_heads(v_mem, N_MEM_HEADS))

        g_sb, g_fx, g_mem = jnp.split(jax.nn.sigmoid(u @ w_gate[l] + b_gate[l]), N_BRANCH, axis=-1)
        merged = (g_sb * (_merge_heads(o_sb) @ w_br_sb[l])
                  + g_fx * (_merge_heads(o_fx) @ w_br_fox[l])
                  + g_mem * (_merge_heads(o_mem) @ w_br_mem[l]))
        h = h + _rmsnorm(merged @ w_out[l], mix_post_g[l])

        f = _swiglu(_rmsnorm(h, ffn2_pre_g[l]), ffn2_w_gate[l], ffn2_w_up[l], ffn2_w_down[l])
        h = h + 0.5 * _rmsnorm(f, ffn2_post_g[l])
    return h
```

```python
import functools

import jax
import jax.numpy as jnp
from jax import lax
from jax.experimental import pallas as pl
from jax.experimental.pallas import tpu as pltpu

F32 = jnp.float32
BF16 = jnp.bfloat16

HEAD_DIM = 64
N_SB_HEADS = 8
N_FOX_HEADS = 8
N_MEM_HEADS = 4
MEM_HEAD_DIM = 128
N_BRANCH = 3
RMS_EPS = 1e-6

LANES = 128
VMEM_LIMIT_BYTES = 56 * 1024 * 1024
ROW_TILE = 512
ATT_TILE = 256
SB_DEAD = 104.0
MASKED = -1e30

_NT = (((1,), (1,)), ((), ()))


def _rms(x, g):
    return x * lax.rsqrt(jnp.mean(x * x, axis=-1, keepdims=True) + RMS_EPS) * g


def _params(*sem):
    return pltpu.CompilerParams(dimension_semantics=sem, vmem_limit_bytes=VMEM_LIMIT_BYTES)


def _resident(shape, index_map):
    return pl.BlockSpec(shape, index_map, pipeline_mode=pl.Buffered(1))


def _ffn_kernel(l_ref, h_ref, gpre_ref, gpost_ref, wg_ref, wu_ref, wd_ref, o_ref):
    x = h_ref[...]
    u = _rms(x, gpre_ref[...]).astype(BF16)
    g = jnp.dot(u, wg_ref[...], preferred_element_type=F32)
    up = jnp.dot(u, wu_ref[...], preferred_element_type=F32)
    a = (g * jax.nn.sigmoid(g) * up).astype(BF16)
    f = jnp.dot(a, wd_ref[...], preferred_element_type=F32)
    o_ref[...] = x + 0.5 * _rms(f, gpost_ref[...])


def _ffn(l, h, g_pre, g_post, w_gate, w_up, w_down):
    m, d = h.shape
    f = w_gate.shape[-1]
    tm = ROW_TILE
    wmap = lambda i, l: (l[0], 0, 0)
    return pl.pallas_call(
        _ffn_kernel,
        grid_spec=pltpu.PrefetchScalarGridSpec(
            num_scalar_prefetch=1,
            grid=(m // tm,),
            in_specs=[
                pl.BlockSpec((tm, d), lambda i, l: (i, 0)),
                _resident((None, 1, d), wmap),
                _resident((None, 1, d), wmap),
                _resident((None, d, f), wmap),
                _resident((None, d, f), wmap),
                _resident((None, f, d), wmap),
            ],
            out_specs=pl.BlockSpec((tm, d), lambda i, l: (i, 0)),
        ),
        out_shape=jax.ShapeDtypeStruct((m, d), F32),
        compiler_params=_params("parallel"),
        name="ffn",
    )(l, h, g_pre, g_post, w_gate, w_up, w_down)


def _proj_kernel(l_ref, h_ref, g_ref, w_ref, wf_ref, bf_ref, qkv_ref, c_ref, carry_ref):
    @pl.when(pl.program_id(1) == 0)
    def _():
        carry_ref[...] = jnp.zeros_like(carry_ref)

    u = _rms(h_ref[...], g_ref[...])
    qkv_ref[...] = jnp.dot(u.astype(BF16), w_ref[...], preferred_element_type=F32).astype(BF16)

    fl = lax.dot_general(wf_ref[...], u, _NT, precision=lax.Precision.HIGHEST,
                         preferred_element_type=F32)
    t = fl + bf_ref[...]
    log_f = jnp.minimum(t, 0.0) - jnp.log1p(jnp.exp(-jnp.abs(t)))
    tm = log_f.shape[1]
    row = lax.broadcasted_iota(jnp.int32, (tm, tm), 0)
    col = lax.broadcasted_iota(jnp.int32, (tm, tm), 1)
    upper = jnp.where(row <= col, 1.0, 0.0).astype(F32)
    c = jnp.dot(log_f, upper, precision=lax.Precision.HIGHEST,
                preferred_element_type=F32) + carry_ref[:, 0:1]
    c_ref[...] = c
    carry_ref[...] = jnp.broadcast_to(c[:, tm - 1:tm], carry_ref.shape)


def _proj(l, h, g, w_main, w_f_t, b_f, batch):
    m, d = h.shape
    seq = m // batch
    tm = ROW_TILE
    nt = seq // tm
    n_out = w_main.shape[-1]
    nh = w_f_t.shape[1]
    wmap = lambda b, i, l: (l[0], 0, 0)
    return pl.pallas_call(
        _proj_kernel,
        grid_spec=pltpu.PrefetchScalarGridSpec(
            num_scalar_prefetch=1,
            grid=(batch, nt),
            in_specs=[
                pl.BlockSpec((tm, d), lambda b, i, l: (b * nt + i, 0)),
                _resident((None, 1, d), wmap),
                _resident((None, d, n_out), wmap),
                _resident((None, nh, d), wmap),
                _resident((None, nh, 1), wmap),
            ],
            out_specs=[
                pl.BlockSpec((tm, n_out), lambda b, i, l: (b * nt + i, 0)),
                pl.BlockSpec((None, nh, tm), lambda b, i, l: (b, 0, i)),
            ],
            scratch_shapes=[pltpu.VMEM((nh, LANES), F32)],
        ),
        out_shape=[
            jax.ShapeDtypeStruct((m, n_out), BF16),
            jax.ShapeDtypeStruct((batch, nh, seq), F32),
        ],
        compiler_params=_params("parallel", "arbitrary"),
        name="proj",
    )(l, h, g, w_main, w_f_t, b_f)


def _softplus(z):
    return jnp.maximum(z, 0.0) + jnp.log1p(jnp.exp(-jnp.abs(z)))


def _sb_kernel(q_ref, k_ref, v_ref, tri_ref, o_ref, acc_ref):
    t = ATT_TILE
    qi = pl.program_id(2)
    lane = lax.broadcasted_iota(jnp.int32, (t, LANES), 1)
    q = q_ref[...]
    zero = jnp.zeros_like(q)
    q_heads = (jnp.where(lane < HEAD_DIM, q, zero), jnp.where(lane >= HEAD_DIM, q, zero))
    tri = tri_ref[...]

    def tile(qh, kb, vb, carry, mask):
        z = lax.dot_general(qh, kb, _NT, preferred_element_type=F32)
        sp = _softplus(z)
        if mask is not None:
            sp = jnp.where(mask, sp, 0.0)
        hi = sp.astype(BF16)
        lo = (sp - hi.astype(F32)).astype(BF16)
        later = jnp.dot(jnp.concatenate([hi, lo], axis=1), tri, preferred_element_type=F32)
        w = jnp.exp((z - sp) - later - carry)
        if mask is not None:
            w = jnp.where(mask, w, 0.0)
        pv = jnp.dot(w.astype(BF16), vb, preferred_element_type=F32)
        return pv, carry + jnp.sum(sp, axis=1, keepdims=True)

    row = lax.broadcasted_iota(jnp.int32, (t, t), 0)
    col = lax.broadcasted_iota(jnp.int32, (t, t), 1)
    strictly_causal = col < row
    start = pl.multiple_of(qi * t, t)
    kb = k_ref[pl.ds(start, t), :]
    vb = v_ref[pl.ds(start, t), :]
    carries = []
    for hh in range(2):
        pv, c = tile(q_heads[hh], kb, vb, jnp.zeros((t, 1), F32), strictly_causal)
        acc_ref[hh] = pv
        carries.append(c)

    def alive(c0, c1):
        return (jnp.minimum(jnp.min(c0), jnp.min(c1)) < SB_DEAD).astype(jnp.int32)

    def cond(state):
        j, live, _, _ = state
        return jnp.logical_and(j >= 0, live > 0)

    def body(state):
        j, _, c0, c1 = state
        start = pl.multiple_of(j * t, t)
        kb = k_ref[pl.ds(start, t), :]
        vb = v_ref[pl.ds(start, t), :]
        pv0, c0 = tile(q_heads[0], kb, vb, c0, None)
        pv1, c1 = tile(q_heads[1], kb, vb, c1, None)
        acc_ref[0] += pv0
        acc_ref[1] += pv1
        return j - 1, alive(c0, c1), c0, c1

    lax.while_loop(cond, body, (qi - 1, alive(*carries), carries[0], carries[1]))
    o_ref[...] = jnp.where(lane < HEAD_DIM, acc_ref[0], acc_ref[1]).astype(o_ref.dtype)


def _sb_attention(qkv, tri, batch, q_col, k_col, v_col):
    m = qkv.shape[0]
    seq = m // batch
    t = ATT_TILE
    nq = seq // t
    n_pairs = N_SB_HEADS * HEAD_DIM // LANES
    return pl.pallas_call(
        _sb_kernel,
        grid=(batch, n_pairs, nq),
        in_specs=[
            pl.BlockSpec((t, LANES), lambda b, p, i: (b * nq + i, q_col + p)),
            pl.BlockSpec((seq, LANES), lambda b, p, i: (b, k_col + p)),
            pl.BlockSpec((seq, LANES), lambda b, p, i: (b, v_col + p)),
            _resident(tri.shape, lambda b, p, i: (0, 0)),
        ],
        out_specs=pl.BlockSpec((t, LANES), lambda b, p, i: (b * nq + i, p)),
        out_shape=jax.ShapeDtypeStruct((m, n_pairs * LANES), BF16),
        scratch_shapes=[pltpu.VMEM((2, t, LANES), F32)],
        compiler_params=_params("parallel", "parallel", "arbitrary"),
        name="sb_attn",
    )(qkv, qkv, qkv, tri)


def _fox_kernel(q_ref, k_ref, v_ref, c_ref, o_ref, acc_ref):
    t = ATT_TILE
    qi = pl.program_id(2)
    lane = lax.broadcasted_iota(jnp.int32, (t, LANES), 1)
    q = q_ref[...]
    zero = jnp.zeros_like(q)
    own = (lane < HEAD_DIM, lane >= HEAD_DIM)
    q_heads = tuple(jnp.where(o, q, zero) for o in own)
    one = jnp.ones((t, LANES), BF16)
    c_here = c_ref[qi]
    c0 = tuple(c_here[hh:hh + 1, 0:1] for hh in range(2))

    def tile(hh, kb, vb, c_blk, m_prev, mask):
        z = lax.dot_general(q_heads[hh], kb, _NT, preferred_element_type=F32)
        y = z + (c0[hh] - c_blk[hh:hh + 1, :])
        if mask is not None:
            y = jnp.where(mask, y, MASKED)
        m_new = jnp.maximum(m_prev, jnp.max(y, axis=1, keepdims=True))
        alpha = jnp.exp(m_prev - m_new)
        p = jnp.exp(y - m_new).astype(BF16)
        pv = jnp.dot(p, jnp.where(own[hh], vb, one), preferred_element_type=F32)
        return alpha, pv, m_new

    row = lax.broadcasted_iota(jnp.int32, (t, t), 0)
    col = lax.broadcasted_iota(jnp.int32, (t, t), 1)
    causal = col <= row
    start = pl.multiple_of(qi * t, t)
    kb = k_ref[pl.ds(start, t), :]
    vb = v_ref[pl.ds(start, t), :]
    ms = []
    for hh in range(2):
        _, pv, m_new = tile(hh, kb, vb, c_here, jnp.full((t, 1), MASKED, F32), causal)
        acc_ref[hh] = pv
        ms.append(m_new)

    def body(step, ms):
        j = qi - 1 - step
        start = pl.multiple_of(j * t, t)
        kb = k_ref[pl.ds(start, t), :]
        vb = v_ref[pl.ds(start, t), :]
        c_blk = c_ref[j]
        out = []
        for hh in range(2):
            alpha, pv, m_new = tile(hh, kb, vb, c_blk, ms[hh], None)
            acc_ref[hh] = alpha * acc_ref[hh] + pv
            out.append(m_new)
        return tuple(out)

    lax.fori_loop(0, qi, body, tuple(ms))
    outs = []
    for hh in range(2):
        acc = acc_ref[hh]
        outs.append(acc / pltpu.roll(acc, HEAD_DIM, axis=1))
    o_ref[...] = jnp.where(own[0], outs[0], outs[1]).astype(o_ref.dtype)


def _fox_attention(qkv, c_tiles, batch, q_col, k_col, v_col):
    m = qkv.shape[0]
    seq = m // batch
    t = ATT_TILE
    nq = seq // t
    n_pairs = N_FOX_HEADS * HEAD_DIM // LANES
    return pl.pallas_call(
        _fox_kernel,
        grid=(batch, n_pairs, nq),
        in_specs=[
            pl.BlockSpec((t, LANES), lambda b, p, i: (b * nq + i, q_col + p)),
            pl.BlockSpec((seq, LANES), lambda b, p, i: (b, k_col + p)),
            pl.BlockSpec((seq, LANES), lambda b, p, i: (b, v_col + p)),
            pl.BlockSpec((None, None, nq, 2, t), lambda b, p, i: (b, p, 0, 0, 0)),
        ],
        out_specs=pl.BlockSpec((t, LANES), lambda b, p, i: (b * nq + i, p)),
        out_shape=jax.ShapeDtypeStruct((m, n_pairs * LANES), BF16),
        scratch_shapes=[pltpu.VMEM((2, t, LANES), F32)],
        compiler_params=_params("parallel", "parallel", "arbitrary"),
        name="fox_attn",
    )(qkv, qkv, qkv, c_tiles)


def _memkv_kernel(mem_ref, g_ref, w_ref, o_ref):
    mem_n = _rms(mem_ref[...], g_ref[...]).astype(BF16)
    o_ref[...] = jnp.dot(mem_n, w_ref[...], preferred_element_type=F32).astype(BF16)


def _memkv(mem2d, g, w_mem_kv):
    n_layers, d, n_out = w_mem_kv.shape
    rows = mem2d.shape[0]
    return pl.pallas_call(
        _memkv_kernel,
        grid=(n_layers,),
        in_specs=[
            _resident((rows, d), lambda l: (0, 0)),
            _resident((1, d), lambda l: (0, 0)),
            pl.BlockSpec((None, d, n_out), lambda l: (l, 0, 0)),
        ],
        out_specs=pl.BlockSpec((None, rows, n_out), lambda l: (l, 0, 0)),
        out_shape=jax.ShapeDtypeStruct((n_layers, rows, n_out), BF16),
        compiler_params=_params("parallel"),
        name="memkv",
    )(mem2d, g, w_mem_kv)


def _post_kernel(l_ref, h_ref, qmem_ref, osb_ref, ofx_ref, kv_ref, gpre_ref, gpost_ref,
                 wgate_ref, bgate_ref, wsb_ref, wfx_ref, wmem_ref, wout_ref, o_ref):
    x = h_ref[...]
    d = x.shape[1]
    u = _rms(x, gpre_ref[...]).astype(BF16)
    gates = jax.nn.sigmoid(jnp.dot(u, wgate_ref[...], preferred_element_type=F32) + bgate_ref[...])

    qm = qmem_ref[...]
    kv = kv_ref[...]
    mem_w = N_MEM_HEADS * MEM_HEAD_DIM
    scale = MEM_HEAD_DIM ** -0.5
    heads = []
    for hh in range(N_MEM_HEADS):
        lo, hi = hh * MEM_HEAD_DIM, (hh + 1) * MEM_HEAD_DIM
        z = lax.dot_general(qm[:, lo:hi], kv[:, lo:hi], _NT, preferred_element_type=F32) * scale
        e = jnp.exp(z - jnp.max(z, axis=1, keepdims=True))
        p = (e / jnp.sum(e, axis=1, keepdims=True)).astype(BF16)
        heads.append(jnp.dot(p, kv[:, mem_w + lo:mem_w + hi], preferred_element_type=F32))
    o_mem = jnp.concatenate(heads, axis=1).astype(BF16)

    merged = (gates[:, 0:d] * jnp.dot(osb_ref[...], wsb_ref[...], preferred_element_type=F32)
              + gates[:, d:2 * d] * jnp.dot(ofx_ref[...], wfx_ref[...], preferred_element_type=F32)
              + gates[:, 2 * d:3 * d] * jnp.dot(o_mem, wmem_ref[...], preferred_element_type=F32))
    y = jnp.dot(merged.astype(BF16), wout_ref[...], preferred_element_type=F32)
    o_ref[...] = x + _rms(y, gpost_ref[...])


def _post(l, h, qkv, o_sb, o_fx, kvmem, g_pre, g_post, w_gate, b_gate, w_sb, w_fx, w_mem, w_out,
          batch, qmem_col):
    m, d = h.shape
    seq = m // batch
    tm = ROW_TILE
    nt = seq // tm
    mem_len = kvmem.shape[1] // batch
    mem_w = N_MEM_HEADS * MEM_HEAD_DIM
    wmap = lambda b, i, l: (l[0], 0, 0)
    rows = lambda b, i, l: (b * nt + i, 0)
    return pl.pallas_call(
        _post_kernel,
        grid_spec=pltpu.PrefetchScalarGridSpec(
            num_scalar_prefetch=1,
            grid=(batch, nt),
            in_specs=[
                pl.BlockSpec((tm, d), rows),
                pl.BlockSpec((tm, mem_w), lambda b, i, l: (b * nt + i, qmem_col)),
                pl.BlockSpec((tm, o_sb.shape[1]), rows),
                pl.BlockSpec((tm, o_fx.shape[1]), rows),
                pl.BlockSpec((None, mem_len, 2 * mem_w), lambda b, i, l: (l[0], b, 0)),
                _resident((None, 1, d), wmap),
                _resident((None, 1, d), wmap),
                _resident((None, d, N_BRANCH * d), wmap),
                _resident((None, 1, N_BRANCH * d), wmap),
                _resident((None,) + w_sb.shape[1:], wmap),
                _resident((None,) + w_fx.shape[1:], wmap),
                _resident((None,) + w_mem.shape[1:], wmap),
                _resident((None, d, d), wmap),
            ],
            out_specs=pl.BlockSpec((tm, d), rows),
        ),
        out_shape=jax.ShapeDtypeStruct((m, d), F32),
        compiler_params=_params("parallel", "parallel"),
        name="post",
    )(l, h, qkv, o_sb, o_fx, kvmem, g_pre, g_post, w_gate, b_gate, w_sb, w_fx, w_mem, w_out)


def kernel(x, mem, ffn1_pre_g, ffn1_post_g, ffn1_w_gate, ffn1_w_up, ffn1_w_down, mix_pre_g, mix_post_g, w_in, b_forget, mem_norm_g, w_mem_kv, w_gate, b_gate, w_br_sb, w_br_fox, w_br_mem, w_out, ffn2_pre_g, ffn2_post_g, ffn2_w_gate, ffn2_w_up, ffn2_w_down):
    batch, seq, d = x.shape
    n_layers = w_in.shape[0]
    sb_w = N_SB_HEADS * HEAD_DIM
    fox_w = N_FOX_HEADS * HEAD_DIM
    mem_w = N_MEM_HEADS * MEM_HEAD_DIM
    assert seq % ROW_TILE == 0 and seq % ATT_TILE == 0 and d % LANES == 0
    assert w_in.shape[2] == 3 * sb_w + 3 * fox_w + N_FOX_HEADS + mem_w

    scale = HEAD_DIM ** -0.5
    f_lo = 3 * sb_w + 3 * fox_w
    f_hi = f_lo + N_FOX_HEADS
    col_scale = jnp.ones((f_lo + mem_w,), F32)
    col_scale = col_scale.at[0:sb_w].set(scale).at[3 * sb_w:3 * sb_w + fox_w].set(scale)
    w_main = (jnp.concatenate([w_in[:, :, :f_lo], w_in[:, :, f_hi:]], axis=2) * col_scale).astype(BF16)
    w_f_t = jnp.swapaxes(w_in[:, :, f_lo:f_hi], 1, 2)
    b_f = b_forget[:, :, None]
    bf = lambda w: w.astype(BF16)
    vec = lambda g: g[:, None, :]
    weights = dict(
        f1=(vec(ffn1_pre_g), vec(ffn1_post_g), bf(ffn1_w_gate), bf(ffn1_w_up), bf(ffn1_w_down)),
        f2=(vec(ffn2_pre_g), vec(ffn2_post_g), bf(ffn2_w_gate), bf(ffn2_w_up), bf(ffn2_w_down)),
        proj=(vec(mix_pre_g), w_main, w_f_t, b_f),
        post=(vec(mix_pre_g), vec(mix_post_g), bf(w_gate), vec(b_gate), bf(w_br_sb), bf(w_br_fox),
              bf(w_br_mem), bf(w_out)),
    )

    t = ATT_TILE
    later = (jnp.arange(t)[:, None] > jnp.arange(t)[None, :]).astype(BF16)
    tri = jnp.concatenate([later, later], axis=0)

    kvmem = _memkv(mem.reshape(batch * mem.shape[1], d), mem_norm_g[None, :], bf(w_mem_kv))

    q_sb_col, k_sb_col, v_sb_col = 0, sb_w // LANES, 2 * sb_w // LANES
    fox0 = 3 * sb_w // LANES
    q_fx_col, k_fx_col, v_fx_col = fox0, fox0 + fox_w // LANES, fox0 + 2 * fox_w // LANES
    qmem_col = f_lo // mem_w
    nq = seq // t

    def layer(h, l):
        l = l[None]
        h = _ffn(l, h, *weights["f1"])
        qkv, c = _proj(l, h, *weights["proj"], batch)
        o_sb = _sb_attention(qkv, tri, batch, q_sb_col, k_sb_col, v_sb_col)
        c_tiles = c.reshape(batch, N_FOX_HEADS // 2, 2, nq, t).transpose(0, 1, 3, 2, 4)
        o_fx = _fox_attention(qkv, c_tiles, batch, q_fx_col, k_fx_col, v_fx_col)
        h = _post(l, h, qkv, o_sb, o_fx, kvmem, *weights["post"], batch, qmem_col)
        h = _ffn(l, h, *weights["f2"])
        return h, None

    h, _ = lax.scan(layer, x.reshape(batch * seq, d), jnp.arange(n_layers, dtype=jnp.int32))
    return h.reshape(batch, seq, d)
```

```python
import functools

import jax
import jax.numpy as jnp
from jax import lax
from jax.experimental import pallas as pl
from jax.experimental.pallas import tpu as pltpu

F32 = jnp.float32
BF16 = jnp.bfloat16

HEAD_DIM = 64
N_SB_HEADS = 8
N_FOX_HEADS = 8
N_MEM_HEADS = 4
MEM_HEAD_DIM = 128
N_BRANCH = 3
RMS_EPS = 1e-6

LANES = 128
VMEM_LIMIT_BYTES = 56 * 1024 * 1024
ROW_TILE = 512
ATT_TILE = 256
EXP_DEAD = 104.0
NORM_SLACK = 1.001
MASKED = -1e30

_NT = (((1,), (1,)), ((), ()))


def _rms(x, g):
    return x * lax.rsqrt(jnp.mean(x * x, axis=-1, keepdims=True) + RMS_EPS) * g


def _params(*sem):
    return pltpu.CompilerParams(dimension_semantics=sem, vmem_limit_bytes=VMEM_LIMIT_BYTES)


def _resident(shape, index_map):
    return pl.BlockSpec(shape, index_map, pipeline_mode=pl.Buffered(1))


def _ffn_kernel(l_ref, h_ref, gpre_ref, gpost_ref, wg_ref, wu_ref, wd_ref, o_ref):
    x = h_ref[...]
    u = _rms(x, gpre_ref[...]).astype(BF16)
    g = jnp.dot(u, wg_ref[...], preferred_element_type=F32)
    up = jnp.dot(u, wu_ref[...], preferred_element_type=F32)
    a = (g * jax.nn.sigmoid(g) * up).astype(BF16)
    f = jnp.dot(a, wd_ref[...], preferred_element_type=F32)
    o_ref[...] = x + 0.5 * _rms(f, gpost_ref[...])


def _ffn(l, h, g_pre, g_post, w_gate, w_up, w_down):
    m, d = h.shape
    f = w_gate.shape[-1]
    tm = ROW_TILE
    wmap = lambda i, l: (l[0], 0, 0)
    return pl.pallas_call(
        _ffn_kernel,
        grid_spec=pltpu.PrefetchScalarGridSpec(
            num_scalar_prefetch=1,
            grid=(m // tm,),
            in_specs=[
                pl.BlockSpec((tm, d), lambda i, l: (i, 0)),
                _resident((None, 1, d), wmap),
                _resident((None, 1, d), wmap),
                _resident((None, d, f), wmap),
                _resident((None, d, f), wmap),
                _resident((None, f, d), wmap),
            ],
            out_specs=pl.BlockSpec((tm, d), lambda i, l: (i, 0)),
        ),
        out_shape=jax.ShapeDtypeStruct((m, d), F32),
        compiler_params=_params("parallel"),
        name="ffn",
    )(l, h, g_pre, g_post, w_gate, w_up, w_down)


def _proj_kernel(l_ref, h_ref, g_ref, w_ref, wf_ref, bf_ref, qkv_ref, c_ref, carry_ref):
    @pl.when(pl.program_id(1) == 0)
    def _():
        carry_ref[...] = jnp.zeros_like(carry_ref)

    u = _rms(h_ref[...], g_ref[...])
    qkv_ref[...] = jnp.dot(u.astype(BF16), w_ref[...], preferred_element_type=F32).astype(BF16)

    fl = lax.dot_general(wf_ref[...], u, _NT, precision=lax.Precision.HIGHEST,
                         preferred_element_type=F32)
    t = fl + bf_ref[...]
    log_f = jnp.minimum(t, 0.0) - jnp.log1p(jnp.exp(-jnp.abs(t)))
    tm = log_f.shape[1]
    row = lax.broadcasted_iota(jnp.int32, (tm, tm), 0)
    col = lax.broadcasted_iota(jnp.int32, (tm, tm), 1)
    upper = jnp.where(row <= col, 1.0, 0.0).astype(F32)
    c = jnp.dot(log_f, upper, precision=lax.Precision.HIGHEST,
                preferred_element_type=F32) + carry_ref[:, 0:1]
    c_ref[...] = c
    carry_ref[...] = jnp.broadcast_to(c[:, tm - 1:tm], carry_ref.shape)


def _proj(l, h, g, w_main, w_f_t, b_f, batch):
    m, d = h.shape
    seq = m // batch
    tm = ROW_TILE
    nt = seq // tm
    n_out = w_main.shape[-1]
    nh = w_f_t.shape[1]
    wmap = lambda b, i, l: (l[0], 0, 0)
    return pl.pallas_call(
        _proj_kernel,
        grid_spec=pltpu.PrefetchScalarGridSpec(
            num_scalar_prefetch=1,
            grid=(batch, nt),
            in_specs=[
                pl.BlockSpec((tm, d), lambda b, i, l: (b * nt + i, 0)),
                _resident((None, 1, d), wmap),
                _resident((None, d, n_out), wmap),
                _resident((None, nh, d), wmap),
                _resident((None, nh, 1), wmap),
            ],
            out_specs=[
                pl.BlockSpec((tm, n_out), lambda b, i, l: (b * nt + i, 0)),
                pl.BlockSpec((None, nh, tm), lambda b, i, l: (b, 0, i)),
            ],
            scratch_shapes=[pltpu.VMEM((nh, LANES), F32)],
        ),
        out_shape=[
            jax.ShapeDtypeStruct((m, n_out), BF16),
            jax.ShapeDtypeStruct((batch, nh, seq), F32),
        ],
        compiler_params=_params("parallel", "arbitrary"),
        name="proj",
    )(l, h, g, w_main, w_f_t, b_f)


def _softplus(z):
    return jnp.maximum(z, 0.0) + jnp.log1p(jnp.exp(-jnp.abs(z)))


def _sb_kernel(q_ref, k_ref, v_ref, tri_ref, o_ref, acc_ref):
    t = ATT_TILE
    qi = pl.program_id(2)
    lane = lax.broadcasted_iota(jnp.int32, (t, LANES), 1)
    q = q_ref[...]
    zero = jnp.zeros_like(q)
    q_heads = (jnp.where(lane < HEAD_DIM, q, zero), jnp.where(lane >= HEAD_DIM, q, zero))
    tri = tri_ref[...]

    row = lax.broadcasted_iota(jnp.int32, (t, t), 0)
    col = lax.broadcasted_iota(jnp.int32, (t, t), 1)
    strictly_causal = col < row

    def scores(hh, j):
        kb = k_ref[pl.ds(pl.multiple_of(j * t, t), t), :]
        return lax.dot_general(q_heads[hh], kb, _NT, preferred_element_type=F32)

    def decay(z, mask):
        sp = _softplus(z)
        if mask is not None:
            sp = jnp.where(mask, sp, 0.0)
        hi = sp.astype(BF16)
        lo = (sp - hi.astype(F32)).astype(BF16)
        later = jnp.dot(jnp.concatenate([hi, lo], axis=1), tri, preferred_element_type=F32)
        return sp, later, jnp.sum(sp, axis=1, keepdims=True)

    def weigh(z, sp, later, carry, mask, j):
        w = jnp.exp((z - sp) - later - carry)
        if mask is not None:
            w = jnp.where(mask, w, 0.0)
        vb = v_ref[pl.ds(pl.multiple_of(j * t, t), t), :]
        return jnp.dot(w.astype(BF16), vb, preferred_element_type=F32)

    jp = jnp.maximum(qi - 1, 0)
    zd = [scores(hh, qi) for hh in range(2)]
    zp = [scores(hh, jp) for hh in range(2)]
    dd = [decay(z, strictly_causal) for z in zd]
    dp = [decay(z, None) for z in zp]
    pvd = [weigh(zd[hh], dd[hh][0], dd[hh][1], 0.0, strictly_causal, qi) for hh in range(2)]
    pvp = [weigh(zp[hh], dp[hh][0], dp[hh][1], dd[hh][2], None, jp) for hh in range(2)]
    has_prev = qi > 0
    carries = []
    for hh in range(2):
        acc_ref[hh] = pvd[hh] + jnp.where(has_prev, pvp[hh], 0.0)
        carries.append(dd[hh][2] + dp[hh][2])

    def alive(c0, c1):
        return (jnp.minimum(jnp.min(c0), jnp.min(c1)) < EXP_DEAD).astype(jnp.int32)

    def cond(state):
        j, live, _, _ = state
        return jnp.logical_and(j >= 0, live > 0)

    def body(state):
        j, _, c0, c1 = state
        cs = (c0, c1)
        zs = [scores(hh, j) for hh in range(2)]
        ds = [decay(z, None) for z in zs]
        for hh in range(2):
            acc_ref[hh] += weigh(zs[hh], ds[hh][0], ds[hh][1], cs[hh], None, j)
        return j - 1, alive(c0 + ds[0][2], c1 + ds[1][2]), c0 + ds[0][2], c1 + ds[1][2]

    lax.while_loop(cond, body, (qi - 2, alive(*carries), carries[0], carries[1]))
    o_ref[...] = jnp.where(lane < HEAD_DIM, acc_ref[0], acc_ref[1]).astype(o_ref.dtype)


def _sb_attention(qkv, tri, batch, q_col, k_col, v_col):
    m = qkv.shape[0]
    seq = m // batch
    t = ATT_TILE
    nq = seq // t
    n_pairs = N_SB_HEADS * HEAD_DIM // LANES
    return pl.pallas_call(
        _sb_kernel,
        grid=(batch, n_pairs, nq),
        in_specs=[
            pl.BlockSpec((t, LANES), lambda b, p, i: (b * nq + i, q_col + p)),
            pl.BlockSpec((seq, LANES), lambda b, p, i: (b, k_col + p)),
            pl.BlockSpec((seq, LANES), lambda b, p, i: (b, v_col + p)),
            _resident(tri.shape, lambda b, p, i: (0, 0)),
        ],
        out_specs=pl.BlockSpec((t, LANES), lambda b, p, i: (b * nq + i, p)),
        out_shape=jax.ShapeDtypeStruct((m, n_pairs * LANES), BF16),
        scratch_shapes=[pltpu.VMEM((2, t, LANES), F32)],
        compiler_params=_params("parallel", "parallel", "arbitrary"),
        name="sb_attn",
    )(qkv, qkv, qkv, tri)


def _fox_kernel(q_ref, k_ref, v_ref, c_ref, o_ref, z_ref, p_ref, acc_ref, kmax_ref):
    t = ATT_TILE
    qi = pl.program_id(2)
    lane = lax.broadcasted_iota(jnp.int32, (t, LANES), 1)
    own = (lane < HEAD_DIM, lane >= HEAD_DIM)

    @pl.when(qi == 0)
    def _():
        kf = k_ref[...].astype(F32)
        k2 = kf * kf
        first = lax.broadcasted_iota(jnp.int32, k2.shape, 1) < HEAD_DIM
        for hh, sel in enumerate((first, jnp.logical_not(first))):
            n2 = jnp.sum(jnp.where(sel, k2, 0.0), axis=1, keepdims=True)
            kmax_ref[hh] = jnp.broadcast_to(jnp.sqrt(jnp.max(n2, axis=0, keepdims=True)),
                                            kmax_ref.shape[1:])

    q = q_ref[...]
    zero = jnp.zeros_like(q)
    q_heads = tuple(jnp.where(o, q, zero) for o in own)
    qf = q.astype(F32)
    q2 = qf * qf
    z_bound = tuple(
        jnp.sqrt(jnp.sum(jnp.where(own[hh], q2, 0.0), axis=1, keepdims=True))
        * (kmax_ref[hh][0:1, 0:1] * NORM_SLACK) for hh in range(2))
    one = jnp.ones((t, LANES), BF16)
    c_here = c_ref[qi]
    c0 = tuple(c_here[hh:hh + 1, 0:1] for hh in range(2))

    def scores(hh, j):
        kb = k_ref[pl.ds(pl.multiple_of(j * t, t), t), :]
        return lax.dot_general(q_heads[hh], kb, _NT, preferred_element_type=F32)

    def softmax_step(hh, z, j, m_prev, mask):
        y = z + (c0[hh] - c_ref[j][hh:hh + 1, :])
        if mask is not None:
            y = jnp.where(mask, y, MASKED)
        m_new = jnp.maximum(m_prev, jnp.max(y, axis=1, keepdims=True))
        return jnp.exp(y - m_new).astype(BF16), jnp.exp(m_prev - m_new), m_new

    def weighted_values(hh, j):
        vb = v_ref[pl.ds(pl.multiple_of(j * t, t), t), :]
        return jnp.dot(p_ref[hh], jnp.where(own[hh], vb, one), preferred_element_type=F32)

    def alive(j_next, ms):
        far = c_ref[j_next]
        dead = [jnp.max(z_bound[hh] + (c0[hh] - far[hh:hh + 1, t - 1:t]) - ms[hh]) < -EXP_DEAD
                for hh in range(2)]
        return jnp.logical_not(jnp.logical_and(dead[0], dead[1])).astype(jnp.int32)

    row = lax.broadcasted_iota(jnp.int32, (t, t), 0)
    col = lax.broadcasted_iota(jnp.int32, (t, t), 1)
    causal = col <= row
    j_next = jnp.maximum(qi - 1, 0)
    zd = [scores(hh, qi) for hh in range(2)]
    zn = [scores(hh, j_next) for hh in range(2)]
    ms, alphas = [], []
    for hh in range(2):
        p, _, m_new = softmax_step(hh, zd[hh], qi, jnp.full((t, 1), MASKED, F32), causal)
        p_ref[hh] = p
        z_ref[hh] = zn[hh]
        acc_ref[hh] = jnp.zeros((t, LANES), F32)
        ms.append(m_new)
        alphas.append(jnp.ones((t, 1), F32))

    def cond(state):
        r, live = state[0], state[1]
        return jnp.logical_and(r <= qi, live > 0)

    def body(state):
        r, _, m0, m1, a0, a1 = state
        ms, alphas = (m0, m1), (a0, a1)
        j = qi - r
        j_next = jnp.maximum(j - 1, 0)
        live = alive(j_next, ms)
        pv = [weighted_values(hh, j + 1) for hh in range(2)]
        zn = [scores(hh, j_next) for hh in range(2)]
        new_m, new_a = [], []
        for hh in range(2):
            p, alpha, m_new = softmax_step(hh, z_ref[hh], j, ms[hh], None)
            acc_ref[hh] = alphas[hh] * acc_ref[hh] + pv[hh]
            p_ref[hh] = p
            z_ref[hh] = zn[hh]
            new_m.append(m_new)
            new_a.append(alpha)
        return r + 1, live, new_m[0], new_m[1], new_a[0], new_a[1]

    state = lax.while_loop(cond, body, (jnp.int32(1), alive(j_next, ms), ms[0], ms[1],
                                        alphas[0], alphas[1]))
    j_last = qi - state[0] + 1
    outs = []
    for hh in range(2):
        acc = state[4 + hh] * acc_ref[hh] + weighted_values(hh, j_last)
        outs.append(acc / pltpu.roll(acc, HEAD_DIM, axis=1))
    o_ref[...] = jnp.where(own[0], outs[0], outs[1]).astype(o_ref.dtype)


def _fox_attention(qkv, c_tiles, batch, q_col, k_col, v_col):
    m = qkv.shape[0]
    seq = m // batch
    t = ATT_TILE
    nq = seq // t
    n_pairs = N_FOX_HEADS * HEAD_DIM // LANES
    return pl.pallas_call(
        _fox_kernel,
        grid=(batch, n_pairs, nq),
        in_specs=[
            pl.BlockSpec((t, LANES), lambda b, p, i: (b * nq + i, q_col + p)),
            pl.BlockSpec((seq, LANES), lambda b, p, i: (b, k_col + p)),
            pl.BlockSpec((seq, LANES), lambda b, p, i: (b, v_col + p)),
            pl.BlockSpec((None, None, nq, 2, t), lambda b, p, i: (b, p, 0, 0, 0)),
        ],
        out_specs=pl.BlockSpec((t, LANES), lambda b, p, i: (b * nq + i, p)),
        out_shape=jax.ShapeDtypeStruct((m, n_pairs * LANES), BF16),
        scratch_shapes=[
            pltpu.VMEM((2, t, t), F32),
            pltpu.VMEM((2, t, t), BF16),
            pltpu.VMEM((2, t, LANES), F32),
            pltpu.VMEM((2, 8, LANES), F32),
        ],
        compiler_params=_params("parallel", "parallel", "arbitrary"),
        name="fox_attn",
    )(qkv, qkv, qkv, c_tiles)


def _memkv_kernel(mem_ref, g_ref, w_ref, o_ref):
    mem_n = _rms(mem_ref[...], g_ref[...]).astype(BF16)
    o_ref[...] = jnp.dot(mem_n, w_ref[...], preferred_element_type=F32).astype(BF16)


def _memkv(mem2d, g, w_mem_kv):
    n_layers, d, n_out = w_mem_kv.shape
    rows = mem2d.shape[0]
    return pl.pallas_call(
        _memkv_kernel,
        grid=(n_layers,),
        in_specs=[
            _resident((rows, d), lambda l: (0, 0)),
            _resident((1, d), lambda l: (0, 0)),
            pl.BlockSpec((None, d, n_out), lambda l: (l, 0, 0)),
        ],
        out_specs=pl.BlockSpec((None, rows, n_out), lambda l: (l, 0, 0)),
        out_shape=jax.ShapeDtypeStruct((n_layers, rows, n_out), BF16),
        compiler_params=_params("parallel"),
        name="memkv",
    )(mem2d, g, w_mem_kv)


def _post_kernel(l_ref, h_ref, qmem_ref, osb_ref, ofx_ref, kv_ref, gpre_ref, gpost_ref,
                 wgate_ref, bgate_ref, wsb_ref, wfx_ref, wmem_ref, wout_ref, o_ref):
    x = h_ref[...]
    d = x.shape[1]
    u = _rms(x, gpre_ref[...]).astype(BF16)
    gates = jax.nn.sigmoid(jnp.dot(u, wgate_ref[...], preferred_element_type=F32) + bgate_ref[...])

    qm = qmem_ref[...]
    kv = kv_ref[...]
    mem_w = N_MEM_HEADS * MEM_HEAD_DIM
    scale = MEM_HEAD_DIM ** -0.5
    heads = []
    for hh in range(N_MEM_HEADS):
        lo, hi = hh * MEM_HEAD_DIM, (hh + 1) * MEM_HEAD_DIM
        z = lax.dot_general(qm[:, lo:hi], kv[:, lo:hi], _NT, preferred_element_type=F32) * scale
        e = jnp.exp(z - jnp.max(z, axis=1, keepdims=True))
        p = (e / jnp.sum(e, axis=1, keepdims=True)).astype(BF16)
        heads.append(jnp.dot(p, kv[:, mem_w + lo:mem_w + hi], preferred_element_type=F32))
    o_mem = jnp.concatenate(heads, axis=1).astype(BF16)

    merged = (gates[:, 0:d] * jnp.dot(osb_ref[...], wsb_ref[...], preferred_element_type=F32)
              + gates[:, d:2 * d] * jnp.dot(ofx_ref[...], wfx_ref[...], preferred_element_type=F32)
              + gates[:, 2 * d:3 * d] * jnp.dot(o_mem, wmem_ref[...], preferred_element_type=F32))
    y = jnp.dot(merged.astype(BF16), wout_ref[...], preferred_element_type=F32)
    o_ref[...] = x + _rms(y, gpost_ref[...])


def _post(l, h, qkv, o_sb, o_fx, kvmem, g_pre, g_post, w_gate, b_gate, w_sb, w_fx, w_mem, w_out,
          batch, qmem_col):
    m, d = h.shape
    seq = m // batch
    tm = ROW_TILE
    nt = seq // tm
    mem_len = kvmem.shape[1] // batch
    mem_w = N_MEM_HEADS * MEM_HEAD_DIM
    wmap = lambda b, i, l: (l[0], 0, 0)
    rows = lambda b, i, l: (b * nt + i, 0)
    return pl.pallas_call(
        _post_kernel,
        grid_spec=pltpu.PrefetchScalarGridSpec(
            num_scalar_prefetch=1,
            grid=(batch, nt),
            in_specs=[
                pl.BlockSpec((tm, d), rows),
                pl.BlockSpec((tm, mem_w), lambda b, i, l: (b * nt + i, qmem_col)),
                pl.BlockSpec((tm, o_sb.shape[1]), rows),
                pl.BlockSpec((tm, o_fx.shape[1]), rows),
                pl.BlockSpec((None, mem_len, 2 * mem_w), lambda b, i, l: (l[0], b, 0)),
                _resident((None, 1, d), wmap),
                _resident((None, 1, d), wmap),
                _resident((None, d, N_BRANCH * d), wmap),
                _resident((None, 1, N_BRANCH * d), wmap),
                _resident((None,) + w_sb.shape[1:], wmap),
                _resident((None,) + w_fx.shape[1:], wmap),
                _resident((None,) + w_mem.shape[1:], wmap),
                _resident((None, d, d), wmap),
            ],
            out_specs=pl.BlockSpec((tm, d), rows),
        ),
        out_shape=jax.ShapeDtypeStruct((m, d), F32),
        compiler_params=_params("parallel", "parallel"),
        name="post",
    )(l, h, qkv, o_sb, o_fx, kvmem, g_pre, g_post, w_gate, b_gate, w_sb, w_fx, w_mem, w_out)


def kernel(x, mem, ffn1_pre_g, ffn1_post_g, ffn1_w_gate, ffn1_w_up, ffn1_w_down, mix_pre_g, mix_post_g, w_in, b_forget, mem_norm_g, w_mem_kv, w_gate, b_gate, w_br_sb, w_br_fox, w_br_mem, w_out, ffn2_pre_g, ffn2_post_g, ffn2_w_gate, ffn2_w_up, ffn2_w_down):
    batch, seq, d = x.shape
    n_layers = w_in.shape[0]
    sb_w = N_SB_HEADS * HEAD_DIM
    fox_w = N_FOX_HEADS * HEAD_DIM
    mem_w = N_MEM_HEADS * MEM_HEAD_DIM
    assert seq % ROW_TILE == 0 and seq % ATT_TILE == 0 and d % LANES == 0
    assert w_in.shape[2] == 3 * sb_w + 3 * fox_w + N_FOX_HEADS + mem_w

    scale = HEAD_DIM ** -0.5
    f_lo = 3 * sb_w + 3 * fox_w
    f_hi = f_lo + N_FOX_HEADS
    col_scale = jnp.ones((f_lo + mem_w,), F32)
    col_scale = col_scale.at[0:sb_w].set(scale).at[3 * sb_w:3 * sb_w + fox_w].set(scale)
    w_main = (jnp.concatenate([w_in[:, :, :f_lo], w_in[:, :, f_hi:]], axis=2) * col_scale).astype(BF16)
    w_f_t = jnp.swapaxes(w_in[:, :, f_lo:f_hi], 1, 2)
    b_f = b_forget[:, :, None]
    bf = lambda w: w.astype(BF16)
    vec = lambda g: g[:, None, :]
    weights = dict(
        f1=(vec(ffn1_pre_g), vec(ffn1_post_g), bf(ffn1_w_gate), bf(ffn1_w_up), bf(ffn1_w_down)),
        f2=(vec(ffn2_pre_g), vec(ffn2_post_g), bf(ffn2_w_gate), bf(ffn2_w_up), bf(ffn2_w_down)),
        proj=(vec(mix_pre_g), w_main, w_f_t, b_f),
        post=(vec(mix_pre_g), vec(mix_post_g), bf(w_gate), vec(b_gate), bf(w_br_sb), bf(w_br_fox),
              bf(w_br_mem), bf(w_out)),
    )

    t = ATT_TILE
    later = (jnp.arange(t)[:, None] > jnp.arange(t)[None, :]).astype(BF16)
    tri = jnp.concatenate([later, later], axis=0)

    kvmem = _memkv(mem.reshape(batch * mem.shape[1], d), mem_norm_g[None, :], bf(w_mem_kv))

    q_sb_col, k_sb_col, v_sb_col = 0, sb_w // LANES, 2 * sb_w // LANES
    fox0 = 3 * sb_w // LANES
    q_fx_col, k_fx_col, v_fx_col = fox0, fox0 + fox_w // LANES, fox0 + 2 * fox_w // LANES
    qmem_col = f_lo // mem_w
    nq = seq // t

    def layer(h, l):
        l = l[None]
        h = _ffn(l, h, *weights["f1"])
        qkv, c = _proj(l, h, *weights["proj"], batch)
        o_sb = _sb_attention(qkv, tri, batch, q_sb_col, k_sb_col, v_sb_col)
        c_tiles = c.reshape(batch, N_FOX_HEADS // 2, 2, nq, t).transpose(0, 1, 3, 2, 4)
        o_fx = _fox_attention(qkv, c_tiles, batch, q_fx_col, k_fx_col, v_fx_col)
        h = _post(l, h, qkv, o_sb, o_fx, kvmem, *weights["post"], batch, qmem_col)
        h = _ffn(l, h, *weights["f2"])
        return h, None

    h, _ = lax.scan(layer, x.reshape(batch * seq, d), jnp.arange(n_layers, dtype=jnp.int32))
    return h.reshape(batch, seq, d)
```

```python
import functools

import jax
import jax.numpy as jnp
from jax import lax
from jax.experimental import pallas as pl
from jax.experimental.pallas import tpu as pltpu

F32 = jnp.float32
BF16 = jnp.bfloat16

HEAD_DIM = 64
N_SB_HEADS = 8
N_FOX_HEADS = 8
N_MEM_HEADS = 4
MEM_HEAD_DIM = 128
N_BRANCH = 3
RMS_EPS = 1e-6

LANES = 128
VMEM_LIMIT_BYTES = 56 * 1024 * 1024
ROW_TILE = 512
ATT_TILE = 256
EXP_DEAD = 104.0
NORM_SLACK = 1.001
MASKED = -1e30

_NT = (((1,), (1,)), ((), ()))


def _rms(x, g):
    return x * lax.rsqrt(jnp.mean(x * x, axis=-1, keepdims=True) + RMS_EPS) * g


def _params(*sem):
    return pltpu.CompilerParams(dimension_semantics=sem, vmem_limit_bytes=VMEM_LIMIT_BYTES)


def _resident(shape, index_map):
    return pl.BlockSpec(shape, index_map, pipeline_mode=pl.Buffered(1))


def _ffn_kernel(l_ref, h_ref, gpre_ref, gpost_ref, wg_ref, wu_ref, wd_ref, o_ref):
    x = h_ref[...]
    u = _rms(x, gpre_ref[...]).astype(BF16)
    g = jnp.dot(u, wg_ref[...], preferred_element_type=F32)
    up = jnp.dot(u, wu_ref[...], preferred_element_type=F32)
    a = (g * jax.nn.sigmoid(g) * up).astype(BF16)
    f = jnp.dot(a, wd_ref[...], preferred_element_type=F32)
    o_ref[...] = x + 0.5 * _rms(f, gpost_ref[...])


def _ffn(l, h, g_pre, g_post, w_gate, w_up, w_down):
    m, d = h.shape
    f = w_gate.shape[-1]
    tm = ROW_TILE
    wmap = lambda i, l: (l[0], 0, 0)
    return pl.pallas_call(
        _ffn_kernel,
        grid_spec=pltpu.PrefetchScalarGridSpec(
            num_scalar_prefetch=1,
            grid=(m // tm,),
            in_specs=[
                pl.BlockSpec((tm, d), lambda i, l: (i, 0)),
                _resident((None, 1, d), wmap),
                _resident((None, 1, d), wmap),
                _resident((None, d, f), wmap),
                _resident((None, d, f), wmap),
                _resident((None, f, d), wmap),
            ],
            out_specs=pl.BlockSpec((tm, d), lambda i, l: (i, 0)),
        ),
        out_shape=jax.ShapeDtypeStruct((m, d), F32),
        compiler_params=_params("parallel"),
        name="ffn",
    )(l, h, g_pre, g_post, w_gate, w_up, w_down)


def _proj_kernel(l_ref, h_ref, g_ref, w_ref, wt_ref, wf_ref, bf_ref, place_ref,
                 rows_ref, cols_ref, c_ref, carry_ref):
    @pl.when(pl.program_id(1) == 0)
    def _():
        carry_ref[...] = jnp.zeros_like(carry_ref)

    u = _rms(h_ref[...], g_ref[...])
    ub = u.astype(BF16)
    main = jnp.dot(ub, w_ref[...], preferred_element_type=F32)
    across = lax.dot_general(wt_ref[...], ub, _NT, preferred_element_type=F32).astype(BF16)
    t = ATT_TILE
    for s in range(cols_ref.shape[0]):
        cols_ref[s] = across[:, s * t:(s + 1) * t]

    fl = lax.dot_general(wf_ref[...], u, _NT, precision=lax.Precision.HIGHEST,
                         preferred_element_type=F32) + bf_ref[...]
    log_f = jnp.minimum(fl, 0.0) - jnp.log1p(jnp.exp(-jnp.abs(fl)))
    nh, tm = log_f.shape
    row = lax.broadcasted_iota(jnp.int32, (tm, tm), 0)
    col = lax.broadcasted_iota(jnp.int32, (tm, tm), 1)
    upper = jnp.where(row <= col, 1.0, 0.0).astype(F32)
    c = jnp.dot(log_f, upper, precision=lax.Precision.HIGHEST,
                preferred_element_type=F32) + carry_ref[:, 0:1]
    c_ref[...] = c
    carry_ref[...] = jnp.broadcast_to(c[:, tm - 1:tm], carry_ref.shape)

    c_hi = c.astype(BF16).astype(F32)
    c_mid = (c - c_hi).astype(BF16).astype(F32)
    c_lo = ((c - c_hi) - c_mid).astype(BF16).astype(F32)
    terms = jnp.concatenate([c_hi, c_mid, c_lo, jnp.zeros((LANES - 3 * nh, tm), F32)], axis=0)
    placed = jnp.dot(terms.T.astype(BF16), place_ref[...], preferred_element_type=F32)

    n_k = N_FOX_HEADS * HEAD_DIM
    pair_w = 3 * LANES
    for p in range(n_k // LANES):
        base = p * pair_w
        rows_ref[:, base:base + LANES] = main[:, p * LANES:(p + 1) * LANES].astype(BF16)
        rows_ref[:, base + LANES:base + pair_w] = (
            placed[:, 2 * p * LANES:2 * (p + 1) * LANES].astype(BF16))
    rows_ref[:, (n_k // LANES) * pair_w:] = main[:, n_k:].astype(BF16)


def _proj(l, h, g, w_main, w_t, w_f_t, b_f, place, batch):
    m, d = h.shape
    seq = m // batch
    tm = ROW_TILE
    nt = seq // tm
    per = tm // ATT_TILE
    nh = w_f_t.shape[1]
    n_k = N_FOX_HEADS * HEAD_DIM
    n_rows_out = w_main.shape[-1] - n_k + (n_k // LANES) * 3 * LANES
    n_cols_out = w_t.shape[1]
    wmap = lambda b, i, l: (l[0], 0, 0)
    return pl.pallas_call(
        _proj_kernel,
        grid_spec=pltpu.PrefetchScalarGridSpec(
            num_scalar_prefetch=1,
            grid=(batch, nt),
            in_specs=[
                pl.BlockSpec((tm, d), lambda b, i, l: (b * nt + i, 0)),
                _resident((None, 1, d), wmap),
                _resident((None, d, w_main.shape[-1]), wmap),
                _resident((None, n_cols_out, d), wmap),
                _resident((None, nh, d), wmap),
                _resident((None, nh, 1), wmap),
                _resident(place.shape, lambda b, i, l: (0, 0)),
            ],
            out_specs=[
                pl.BlockSpec((tm, n_rows_out), lambda b, i, l: (b * nt + i, 0)),
                pl.BlockSpec((None, per, n_cols_out, ATT_TILE), lambda b, i, l: (b, i, 0, 0)),
                pl.BlockSpec((None, nh, tm), lambda b, i, l: (b, 0, i)),
            ],
            scratch_shapes=[pltpu.VMEM((nh, LANES), F32)],
        ),
        out_shape=[
            jax.ShapeDtypeStruct((m, n_rows_out), BF16),
            jax.ShapeDtypeStruct((batch, seq // ATT_TILE, n_cols_out, ATT_TILE), BF16),
            jax.ShapeDtypeStruct((batch, nh, seq), F32),
        ],
        compiler_params=_params("parallel", "arbitrary"),
        name="proj",
    )(l, h, g, w_main, w_t, w_f_t, b_f, place)


def _softplus(z):
    return jnp.maximum(z, 0.0) + jnp.log(1.0 + jnp.exp(-jnp.abs(z)))


def _sb_kernel(q_ref, k_ref, v_ref, tri_ref, o_ref, acc_ref):
    t = ATT_TILE
    qi = pl.program_id(2)
    lane = lax.broadcasted_iota(jnp.int32, (t, LANES), 1)
    q = q_ref[...]
    zero = jnp.zeros_like(q)
    q_heads = (jnp.where(lane < HEAD_DIM, q, zero), jnp.where(lane >= HEAD_DIM, q, zero))
    tri = tri_ref[...]

    row = lax.broadcasted_iota(jnp.int32, (t, t), 0)
    col = lax.broadcasted_iota(jnp.int32, (t, t), 1)
    strictly_causal = col < row

    def scores(hh, j):
        kb = k_ref[pl.ds(pl.multiple_of(j * t, t), t), :]
        return lax.dot_general(q_heads[hh], kb, _NT, preferred_element_type=F32)

    def decay(z, mask):
        sp = _softplus(z)
        if mask is not None:
            sp = jnp.where(mask, sp, 0.0)
        hi = sp.astype(BF16)
        lo = (sp - hi.astype(F32)).astype(BF16)
        later = jnp.dot(jnp.concatenate([hi, lo], axis=1), tri, preferred_element_type=F32)
        return sp, later, jnp.sum(sp, axis=1, keepdims=True)

    def weigh(z, sp, later, carry, mask, j):
        w = jnp.exp((z - sp) - later - carry)
        if mask is not None:
            w = jnp.where(mask, w, 0.0)
        vb = v_ref[pl.ds(pl.multiple_of(j * t, t), t), :]
        return jnp.dot(w.astype(BF16), vb, preferred_element_type=F32)

    jp = jnp.maximum(qi - 1, 0)
    zd = [scores(hh, qi) for hh in range(2)]
    zp = [scores(hh, jp) for hh in range(2)]
    dd = [decay(z, strictly_causal) for z in zd]
    dp = [decay(z, None) for z in zp]
    pvd = [weigh(zd[hh], dd[hh][0], dd[hh][1], 0.0, strictly_causal, qi) for hh in range(2)]
    pvp = [weigh(zp[hh], dp[hh][0], dp[hh][1], dd[hh][2], None, jp) for hh in range(2)]
    has_prev = qi > 0
    carries = []
    for hh in range(2):
        acc_ref[hh] = pvd[hh] + jnp.where(has_prev, pvp[hh], 0.0)
        carries.append(dd[hh][2] + dp[hh][2])

    def alive(c0, c1):
        return (jnp.minimum(jnp.min(c0), jnp.min(c1)) < EXP_DEAD).astype(jnp.int32)

    def cond(state):
        j, live, _, _ = state
        return jnp.logical_and(j >= 0, live > 0)

    def body(state):
        j, _, c0, c1 = state
        cs = (c0, c1)
        zs = [scores(hh, j) for hh in range(2)]
        ds = [decay(z, None) for z in zs]
        for hh in range(2):
            acc_ref[hh] += weigh(zs[hh], ds[hh][0], ds[hh][1], cs[hh], None, j)
        return j - 1, alive(c0 + ds[0][2], c1 + ds[1][2]), c0 + ds[0][2], c1 + ds[1][2]

    lax.while_loop(cond, body, (qi - 2, alive(*carries), carries[0], carries[1]))
    o_ref[...] = jnp.where(lane < HEAD_DIM, acc_ref[0], acc_ref[1]).astype(o_ref.dtype)


def _sb_attention(qkv, tri, batch, q_col, k_col, v_col):
    m = qkv.shape[0]
    seq = m // batch
    t = ATT_TILE
    nq = seq // t
    n_pairs = N_SB_HEADS * HEAD_DIM // LANES
    return pl.pallas_call(
        _sb_kernel,
        grid=(batch, n_pairs, nq),
        in_specs=[
            pl.BlockSpec((t, LANES), lambda b, p, i: (b * nq + i, q_col + p)),
            pl.BlockSpec((seq, LANES), lambda b, p, i: (b, k_col + p)),
            pl.BlockSpec((seq, LANES), lambda b, p, i: (b, v_col + p)),
            _resident(tri.shape, lambda b, p, i: (0, 0)),
        ],
        out_specs=pl.BlockSpec((t, LANES), lambda b, p, i: (b * nq + i, p)),
        out_shape=jax.ShapeDtypeStruct((m, n_pairs * LANES), BF16),
        scratch_shapes=[pltpu.VMEM((2, t, LANES), F32)],
        compiler_params=_params("parallel", "parallel", "arbitrary"),
        name="sb_attn",
    )(qkv, qkv, qkv, tri)


def _fox_kernel(qt_ref, k_ref, vt_ref, cend_ref, o_ref, y_ref, p_ref, acc_ref, kmax_ref):
    t = ATT_TILE
    qi = pl.program_id(2)
    sub = lax.broadcasted_iota(jnp.int32, (LANES, t), 0)
    own = (sub < HEAD_DIM, sub >= HEAD_DIM)

    @pl.when(qi == 0)
    def _():
        kf = k_ref[:, 0:LANES].astype(F32)
        k2 = kf * kf
        first = lax.broadcasted_iota(jnp.int32, k2.shape, 1) < HEAD_DIM
        for hh, sel in enumerate((first, jnp.logical_not(first))):
            n2 = jnp.sum(jnp.where(sel, k2, 0.0), axis=1, keepdims=True)
            kmax_ref[hh] = jnp.broadcast_to(jnp.sqrt(jnp.max(n2, axis=0, keepdims=True)),
                                            kmax_ref.shape[1:])

    q_pair = qt_ref[...]
    qf = q_pair.astype(F32)
    q2 = qf * qf
    minus_c = jnp.where(sub < 3, -1.0, 0.0).astype(BF16)
    q_aug, z_bound = [], []
    for hh in range(2):
        z_bound.append(jnp.sqrt(jnp.sum(jnp.where(own[hh], q2, 0.0), axis=0, keepdims=True))
                       * (kmax_ref[hh][0:1, 0:1] * NORM_SLACK))
        q_aug.append(jnp.concatenate([jnp.where(own[hh], q_pair, jnp.zeros_like(q_pair)), minus_c],
                                     axis=0))
    one = jnp.ones((LANES, t), BF16)

    def logits(hh, j):
        kb = k_ref[pl.ds(pl.multiple_of(j * t, t), t), :]
        lhs = kb[:, :2 * LANES] if hh == 0 else jnp.concatenate(
            [kb[:, :LANES], kb[:, 2 * LANES:]], axis=1)
        return jnp.dot(lhs, q_aug[hh], preferred_element_type=F32)

    def softmax_step(y, m_prev, mask):
        if mask is not None:
            y = jnp.where(mask, y, MASKED)
        m_new = jnp.maximum(m_prev, jnp.max(y, axis=0, keepdims=True))
        return jnp.exp(y - m_new).astype(BF16), jnp.exp(m_prev - m_new), m_new

    def weighted_values(hh, j):
        vt = jnp.where(own[hh], vt_ref[j], one)
        return jnp.dot(vt, p_ref[hh], preferred_element_type=F32)

    def alive(j_next, ms):
        far = cend_ref[pl.ds(j_next, 1), :]
        dead = [jnp.max(z_bound[hh] - far[:, hh:hh + 1] - ms[hh]) < -EXP_DEAD for hh in range(2)]
        return jnp.logical_not(jnp.logical_and(dead[0], dead[1])).astype(jnp.int32)

    key = lax.broadcasted_iota(jnp.int32, (t, t), 0)
    query = lax.broadcasted_iota(jnp.int32, (t, t), 1)
    causal = key <= query
    j_next = jnp.maximum(qi - 1, 0)
    yd = [logits(hh, qi) for hh in range(2)]
    yn = [logits(hh, j_next) for hh in range(2)]
    ms, alphas = [], []
    for hh in range(2):
        p, _, m_new = softmax_step(yd[hh], jnp.full((1, t), MASKED, F32), causal)
        p_ref[hh] = p
        y_ref[hh] = yn[hh]
        acc_ref[hh] = jnp.zeros((LANES, t), F32)
        ms.append(m_new)
        alphas.append(jnp.ones((1, t), F32))

    def cond(state):
        r, live = state[0], state[1]
        return jnp.logical_and(r <= qi, live > 0)

    def body(state):
        r, _, m0, m1, a0, a1 = state
        ms, alphas = (m0, m1), (a0, a1)
        j = qi - r
        j_next = jnp.maximum(j - 1, 0)
        live = alive(j_next, ms)
        pv = [weighted_values(hh, j + 1) for hh in range(2)]
        yn = [logits(hh, j_next) for hh in range(2)]
        new_m, new_a = [], []
        for hh in range(2):
            p, alpha, m_new = softmax_step(y_ref[hh], ms[hh], None)
            acc_ref[hh] = alphas[hh] * acc_ref[hh] + pv[hh]
            p_ref[hh] = p
            y_ref[hh] = yn[hh]
            new_m.append(m_new)
            new_a.append(alpha)
        return r + 1, live, new_m[0], new_m[1], new_a[0], new_a[1]

    state = lax.while_loop(cond, body, (jnp.int32(1), alive(j_next, ms), ms[0], ms[1],
                                        alphas[0], alphas[1]))
    j_last = qi - state[0] + 1
    outs = []
    for hh in range(2):
        acc = state[4 + hh] * acc_ref[hh] + weighted_values(hh, j_last)
        outs.append(acc / pltpu.roll(acc, HEAD_DIM, axis=0))
    o_ref[...] = jnp.where(own[0], outs[0], outs[1]).T.astype(o_ref.dtype)


def _fox_attention(rows, cols, c_ends, batch):
    m = rows.shape[0]
    seq = m // batch
    t = ATT_TILE
    nq = seq // t
    n_pairs = N_FOX_HEADS * HEAD_DIM // LANES
    return pl.pallas_call(
        _fox_kernel,
        grid=(batch, n_pairs, nq),
        in_specs=[
            pl.BlockSpec((None, None, LANES, t), lambda b, p, i: (b, i, p, 0)),
            pl.BlockSpec((seq, 3 * LANES), lambda b, p, i: (b, p)),
            pl.BlockSpec((None, nq, LANES, t), lambda b, p, i: (b, 0, n_pairs + p, 0)),
            pl.BlockSpec((None, None, nq, 2), lambda b, p, i: (b, p, 0, 0)),
        ],
        out_specs=pl.BlockSpec((t, LANES), lambda b, p, i: (b * nq + i, p)),
        out_shape=jax.ShapeDtypeStruct((m, n_pairs * LANES), BF16),
        scratch_shapes=[
            pltpu.VMEM((2, t, t), F32),
            pltpu.VMEM((2, t, t), BF16),
            pltpu.VMEM((2, LANES, t), F32),
            pltpu.VMEM((2, 8, LANES), F32),
        ],
        compiler_params=_params("parallel", "parallel", "arbitrary"),
        name="fox_attn",
    )(cols, rows, cols, c_ends)


def _memkv_kernel(mem_ref, g_ref, w_ref, o_ref):
    mem_n = _rms(mem_ref[...], g_ref[...]).astype(BF16)
    o_ref[...] = jnp.dot(mem_n, w_ref[...], preferred_element_type=F32).astype(BF16)


def _memkv(mem2d, g, w_mem_kv):
    n_layers, d, n_out = w_mem_kv.shape
    rows = mem2d.shape[0]
    return pl.pallas_call(
        _memkv_kernel,
        grid=(n_layers,),
        in_specs=[
            _resident((rows, d), lambda l: (0, 0)),
            _resident((1, d), lambda l: (0, 0)),
            pl.BlockSpec((None, d, n_out), lambda l: (l, 0, 0)),
        ],
        out_specs=pl.BlockSpec((None, rows, n_out), lambda l: (l, 0, 0)),
        out_shape=jax.ShapeDtypeStruct((n_layers, rows, n_out), BF16),
        compiler_params=_params("parallel"),
        name="memkv",
    )(mem2d, g, w_mem_kv)


def _post_kernel(l_ref, h_ref, qmem_ref, osb_ref, ofx_ref, kv_ref, gpre_ref, gpost_ref,
                 wgate_ref, bgate_ref, wsb_ref, wfx_ref, wmem_ref, wout_ref, o_ref):
    x = h_ref[...]
    d = x.shape[1]
    u = _rms(x, gpre_ref[...]).astype(BF16)
    gates = jax.nn.sigmoid(jnp.dot(u, wgate_ref[...], preferred_element_type=F32) + bgate_ref[...])

    qm = qmem_ref[...]
    kv = kv_ref[...]
    mem_w = N_MEM_HEADS * MEM_HEAD_DIM
    scale = MEM_HEAD_DIM ** -0.5
    heads = []
    for hh in range(N_MEM_HEADS):
        lo, hi = hh * MEM_HEAD_DIM, (hh + 1) * MEM_HEAD_DIM
        z = lax.dot_general(qm[:, lo:hi], kv[:, lo:hi], _NT, preferred_element_type=F32) * scale
        e = jnp.exp(z - jnp.max(z, axis=1, keepdims=True))
        p = (e / jnp.sum(e, axis=1, keepdims=True)).astype(BF16)
        heads.append(jnp.dot(p, kv[:, mem_w + lo:mem_w + hi], preferred_element_type=F32))
    o_mem = jnp.concatenate(heads, axis=1).astype(BF16)

    merged = (gates[:, 0:d] * jnp.dot(osb_ref[...], wsb_ref[...], preferred_element_type=F32)
              + gates[:, d:2 * d] * jnp.dot(ofx_ref[...], wfx_ref[...], preferred_element_type=F32)
              + gates[:, 2 * d:3 * d] * jnp.dot(o_mem, wmem_ref[...], preferred_element_type=F32))
    y = jnp.dot(merged.astype(BF16), wout_ref[...], preferred_element_type=F32)
    o_ref[...] = x + _rms(y, gpost_ref[...])


def _post(l, h, qkv, o_sb, o_fx, kvmem, g_pre, g_post, w_gate, b_gate, w_sb, w_fx, w_mem, w_out,
          batch, qmem_col):
    m, d = h.shape
    seq = m // batch
    tm = ROW_TILE
    nt = seq // tm
    mem_len = kvmem.shape[1] // batch
    mem_w = N_MEM_HEADS * MEM_HEAD_DIM
    wmap = lambda b, i, l: (l[0], 0, 0)
    rows = lambda b, i, l: (b * nt + i, 0)
    return pl.pallas_call(
        _post_kernel,
        grid_spec=pltpu.PrefetchScalarGridSpec(
            num_scalar_prefetch=1,
            grid=(batch, nt),
            in_specs=[
                pl.BlockSpec((tm, d), rows),
                pl.BlockSpec((tm, mem_w), lambda b, i, l: (b * nt + i, qmem_col)),
                pl.BlockSpec((tm, o_sb.shape[1]), rows),
                pl.BlockSpec((tm, o_fx.shape[1]), rows),
                pl.BlockSpec((None, mem_len, 2 * mem_w), lambda b, i, l: (l[0], b, 0)),
                _resident((None, 1, d), wmap),
                _resident((None, 1, d), wmap),
                _resident((None, d, N_BRANCH * d), wmap),
                _resident((None, 1, N_BRANCH * d), wmap),
                _resident((None,) + w_sb.shape[1:], wmap),
                _resident((None,) + w_fx.shape[1:], wmap),
                _resident((None,) + w_mem.shape[1:], wmap),
                _resident((None, d, d), wmap),
            ],
            out_specs=pl.BlockSpec((tm, d), rows),
        ),
        out_shape=jax.ShapeDtypeStruct((m, d), F32),
        compiler_params=_params("parallel", "parallel"),
        name="post",
    )(l, h, qkv, o_sb, o_fx, kvmem, g_pre, g_post, w_gate, b_gate, w_sb, w_fx, w_mem, w_out)


def kernel(x, mem, ffn1_pre_g, ffn1_post_g, ffn1_w_gate, ffn1_w_up, ffn1_w_down, mix_pre_g, mix_post_g, w_in, b_forget, mem_norm_g, w_mem_kv, w_gate, b_gate, w_br_sb, w_br_fox, w_br_mem, w_out, ffn2_pre_g, ffn2_post_g, ffn2_w_gate, ffn2_w_up, ffn2_w_down):
    batch, seq, d = x.shape
    n_layers = w_in.shape[0]
    sb_w = N_SB_HEADS * HEAD_DIM
    fox_w = N_FOX_HEADS * HEAD_DIM
    mem_w = N_MEM_HEADS * MEM_HEAD_DIM
    assert seq % ROW_TILE == 0 and seq % ATT_TILE == 0 and d % LANES == 0
    assert w_in.shape[2] == 3 * sb_w + 3 * fox_w + N_FOX_HEADS + mem_w

    scale = HEAD_DIM ** -0.5
    fx = 3 * sb_w
    f_lo = fx + 3 * fox_w
    f_hi = f_lo + N_FOX_HEADS
    w_main = jnp.concatenate([w_in[:, :, fx + fox_w:fx + 2 * fox_w], w_in[:, :, :sb_w] * scale,
                              w_in[:, :, sb_w:fx], w_in[:, :, f_hi:]], axis=2).astype(BF16)
    w_t = jnp.swapaxes(jnp.concatenate([w_in[:, :, fx:fx + fox_w] * scale,
                                        w_in[:, :, fx + 2 * fox_w:f_lo]], axis=2), 1, 2).astype(BF16)
    w_f_t = jnp.swapaxes(w_in[:, :, f_lo:f_hi], 1, 2)
    b_f = b_forget[:, :, None]
    term, head = jnp.meshgrid(jnp.arange(3), jnp.arange(N_FOX_HEADS), indexing="ij")
    place = jnp.zeros((LANES, N_FOX_HEADS * LANES), BF16).at[
        term * N_FOX_HEADS + head, head * LANES + term].set(1.0)
    bf = lambda w: w.astype(BF16)
    vec = lambda g: g[:, None, :]
    weights = dict(
        f1=(vec(ffn1_pre_g), vec(ffn1_post_g), bf(ffn1_w_gate), bf(ffn1_w_up), bf(ffn1_w_down)),
        f2=(vec(ffn2_pre_g), vec(ffn2_post_g), bf(ffn2_w_gate), bf(ffn2_w_up), bf(ffn2_w_down)),
        proj=(vec(mix_pre_g), w_main, w_t, w_f_t, b_f, place),
        post=(vec(mix_pre_g), vec(mix_post_g), bf(w_gate), vec(b_gate), bf(w_br_sb), bf(w_br_fox),
              bf(w_br_mem), bf(w_out)),
    )

    t = ATT_TILE
    later = (jnp.arange(t)[:, None] > jnp.arange(t)[None, :]).astype(BF16)
    tri = jnp.concatenate([later, later], axis=0)

    kvmem = _memkv(mem.reshape(batch * mem.shape[1], d), mem_norm_g[None, :], bf(w_mem_kv))

    fox_k_w = (fox_w // LANES) * 3 * LANES
    q_sb_col = fox_k_w // LANES
    k_sb_col, v_sb_col = q_sb_col + sb_w // LANES, q_sb_col + 2 * sb_w // LANES
    qmem_col = (fox_k_w + fx) // mem_w
    nq = seq // t

    def layer(h, l):
        l = l[None]
        h = _ffn(l, h, *weights["f1"])
        rows, cols, c = _proj(l, h, *weights["proj"], batch)
        o_sb = _sb_attention(rows, tri, batch, q_sb_col, k_sb_col, v_sb_col)
        c_ends = c[:, :, t - 1::t].reshape(batch, N_FOX_HEADS // 2, 2, nq).transpose(0, 1, 3, 2)
        o_fx = _fox_attention(rows, cols, c_ends, batch)
        h = _post(l, h, rows, o_sb, o_fx, kvmem, *weights["post"], batch, qmem_col)
        h = _ffn(l, h, *weights["f2"])
        return h, None

    h, _ = lax.scan(layer, x.reshape(batch * seq, d), jnp.arange(n_layers, dtype=jnp.int32))
    return h.reshape(batch, seq, d)
```

```python
import functools

import jax
import jax.numpy as jnp
from jax import lax
from jax.experimental import pallas as pl
from jax.experimental.pallas import tpu as pltpu

F32 = jnp.float32
BF16 = jnp.bfloat16

HEAD_DIM = 64
N_SB_HEADS = 8
N_FOX_HEADS = 8
N_MEM_HEADS = 4
MEM_HEAD_DIM = 128
N_BRANCH = 3
RMS_EPS = 1e-6

LANES = 128
VMEM_LIMIT_BYTES = 56 * 1024 * 1024
ROW_TILE = 512
ATT_TILE = 256
FOX_SUB_TILES = 2
EXP_DEAD = 104.0
NORM_SLACK = 1.001
MASKED = -1e30

_NT = (((1,), (1,)), ((), ()))


def _rms(x, g):
    return x * lax.rsqrt(jnp.mean(x * x, axis=-1, keepdims=True) + RMS_EPS) * g


def _params(*sem):
    return pltpu.CompilerParams(dimension_semantics=sem, vmem_limit_bytes=VMEM_LIMIT_BYTES)


def _resident(shape, index_map):
    return pl.BlockSpec(shape, index_map, pipeline_mode=pl.Buffered(1))


def _ffn_kernel(l_ref, h_ref, gpre_ref, gpost_ref, wg_ref, wu_ref, wd_ref, o_ref):
    x = h_ref[...]
    u = _rms(x, gpre_ref[...]).astype(BF16)
    g = jnp.dot(u, wg_ref[...], preferred_element_type=F32)
    up = jnp.dot(u, wu_ref[...], preferred_element_type=F32)
    a = (g * jax.nn.sigmoid(g) * up).astype(BF16)
    f = jnp.dot(a, wd_ref[...], preferred_element_type=F32)
    o_ref[...] = x + 0.5 * _rms(f, gpost_ref[...])


def _ffn(l, h, g_pre, g_post, w_gate, w_up, w_down):
    m, d = h.shape
    f = w_gate.shape[-1]
    tm = ROW_TILE
    wmap = lambda i, l: (l[0], 0, 0)
    return pl.pallas_call(
        _ffn_kernel,
        grid_spec=pltpu.PrefetchScalarGridSpec(
            num_scalar_prefetch=1,
            grid=(m // tm,),
            in_specs=[
                pl.BlockSpec((tm, d), lambda i, l: (i, 0)),
                _resident((None, 1, d), wmap),
                _resident((None, 1, d), wmap),
                _resident((None, d, f), wmap),
                _resident((None, d, f), wmap),
                _resident((None, f, d), wmap),
            ],
            out_specs=pl.BlockSpec((tm, d), lambda i, l: (i, 0)),
        ),
        out_shape=jax.ShapeDtypeStruct((m, d), F32),
        compiler_params=_params("parallel"),
        name="ffn",
    )(l, h, g_pre, g_post, w_gate, w_up, w_down)


def _split3(x):
    hi = x.astype(BF16).astype(F32)
    mid = (x - hi).astype(BF16).astype(F32)
    lo = ((x - hi) - mid).astype(BF16).astype(F32)
    return hi, mid, lo


def _fold3(x, nh):
    n = x.shape[1]
    return x + pltpu.roll(x, n - nh, axis=1) + pltpu.roll(x, n - 2 * nh, axis=1)


def _spread3(terms, nh):
    lane = lax.broadcasted_iota(jnp.int32, terms[0].shape, 1)
    out = jnp.where(lane < nh, terms[0], 0.0)
    for i in (1, 2):
        moved = pltpu.roll(terms[i], i * nh, axis=1)
        out = jnp.where(jnp.logical_and(lane >= i * nh, lane < (i + 1) * nh), moved, out)
    return out.astype(BF16)


def _proj_kernel(l_ref, h_ref, g_ref, w_ref, wt_ref, wg_ref, bf_ref, lower_ref, place_ref,
                 rows_ref, cols_ref, c_ref, carry_ref):
    @pl.when(pl.program_id(1) == 0)
    def _():
        carry_ref[...] = jnp.zeros_like(carry_ref)

    nh = N_FOX_HEADS
    u = _rms(h_ref[...], g_ref[...])
    u_hi, u_mid, u_lo = _split3(u)
    ub = u_hi.astype(BF16)

    w_gate3 = wg_ref[...]
    e_hi = jnp.dot(ub, w_gate3, preferred_element_type=F32)
    e_mid = jnp.dot(u_mid.astype(BF16), w_gate3, preferred_element_type=F32)
    e_lo = jnp.dot(u_lo.astype(BF16), w_gate3, preferred_element_type=F32)
    drop_lo = lax.broadcasted_iota(jnp.int32, e_mid.shape, 1) < 2 * nh
    fl = _fold3(e_hi + jnp.where(drop_lo, e_mid, 0.0), nh) + e_lo + bf_ref[...]
    log_f = jnp.minimum(fl, 0.0) - jnp.log1p(jnp.exp(-jnp.abs(fl)))

    sums = jnp.dot(lower_ref[...], _spread3(_split3(log_f), nh), preferred_element_type=F32)
    c = _fold3(sums, nh) + carry_ref[0:1, :]
    tm = c.shape[0]
    c_ref[...] = c
    carry_ref[...] = jnp.broadcast_to(c[tm - 1:tm, :], carry_ref.shape)
    placed = jnp.dot(_spread3(_split3(c), nh), place_ref[...], preferred_element_type=F32)

    main = jnp.dot(ub, w_ref[...], preferred_element_type=F32)
    n_k = N_FOX_HEADS * HEAD_DIM
    pair_w = 3 * LANES
    for p in range(n_k // LANES):
        base = p * pair_w
        rows_ref[:, base:base + LANES] = main[:, p * LANES:(p + 1) * LANES].astype(BF16)
        rows_ref[:, base + LANES:base + pair_w] = (
            placed[:, 2 * p * LANES:2 * (p + 1) * LANES].astype(BF16))
    rows_ref[:, (n_k // LANES) * pair_w:] = main[:, n_k:].astype(BF16)

    across = lax.dot_general(wt_ref[...], ub, _NT, preferred_element_type=F32).astype(BF16)
    t = ATT_TILE
    for s in range(cols_ref.shape[0]):
        cols_ref[s] = across[:, s * t:(s + 1) * t]


def _proj(l, h, g, w_main, w_t, w_gate3, b_f, lower, place, batch):
    m, d = h.shape
    seq = m // batch
    tm = ROW_TILE
    nt = seq // tm
    per = tm // ATT_TILE
    n_k = N_FOX_HEADS * HEAD_DIM
    n_rows_out = w_main.shape[-1] - n_k + (n_k // LANES) * 3 * LANES
    n_cols_out = w_t.shape[1]
    wmap = lambda b, i, l: (l[0], 0, 0)
    const = lambda b, i, l: (0, 0)
    return pl.pallas_call(
        _proj_kernel,
        grid_spec=pltpu.PrefetchScalarGridSpec(
            num_scalar_prefetch=1,
            grid=(batch, nt),
            in_specs=[
                pl.BlockSpec((tm, d), lambda b, i, l: (b * nt + i, 0)),
                _resident((None, 1, d), wmap),
                _resident((None, d, w_main.shape[-1]), wmap),
                _resident((None, n_cols_out, d), wmap),
                _resident((None, d, LANES), wmap),
                _resident((None, 1, LANES), wmap),
                _resident(lower.shape, const),
                _resident(place.shape, const),
            ],
            out_specs=[
                pl.BlockSpec((tm, n_rows_out), lambda b, i, l: (b * nt + i, 0)),
                pl.BlockSpec((None, per, n_cols_out, ATT_TILE), lambda b, i, l: (b, i, 0, 0)),
                pl.BlockSpec((tm, LANES), lambda b, i, l: (b * nt + i, 0)),
            ],
            scratch_shapes=[pltpu.VMEM((8, LANES), F32)],
        ),
        out_shape=[
            jax.ShapeDtypeStruct((m, n_rows_out), BF16),
            jax.ShapeDtypeStruct((batch, seq // ATT_TILE, n_cols_out, ATT_TILE), BF16),
            jax.ShapeDtypeStruct((m, LANES), F32),
        ],
        compiler_params=_params("parallel", "arbitrary"),
        name="proj",
    )(l, h, g, w_main, w_t, w_gate3, b_f, lower, place)


def _softplus(z):
    return jnp.maximum(z, 0.0) + jnp.log(1.0 + jnp.exp(-jnp.abs(z)))


def _sb_kernel(q_ref, k_ref, v_ref, tri_ref, o_ref, acc_ref):
    t = ATT_TILE
    qi = pl.program_id(2)
    lane = lax.broadcasted_iota(jnp.int32, (t, LANES), 1)
    q = q_ref[...]
    zero = jnp.zeros_like(q)
    q_heads = (jnp.where(lane < HEAD_DIM, q, zero), jnp.where(lane >= HEAD_DIM, q, zero))
    tri = tri_ref[...]

    row = lax.broadcasted_iota(jnp.int32, (t, t), 0)
    col = lax.broadcasted_iota(jnp.int32, (t, t), 1)
    strictly_causal = col < row

    def scores(hh, j):
        kb = k_ref[pl.ds(pl.multiple_of(j * t, t), t), :]
        return lax.dot_general(q_heads[hh], kb, _NT, preferred_element_type=F32)

    def decay(z, mask):
        sp = _softplus(z)
        if mask is not None:
            sp = jnp.where(mask, sp, 0.0)
        hi = sp.astype(BF16)
        lo = (sp - hi.astype(F32)).astype(BF16)
        later = jnp.dot(jnp.concatenate([hi, lo], axis=1), tri, preferred_element_type=F32)
        return sp, later, jnp.sum(sp, axis=1, keepdims=True)

    def weigh(z, sp, later, carry, mask, j):
        w = jnp.exp((z - sp) - later - carry)
        if mask is not None:
            w = jnp.where(mask, w, 0.0)
        vb = v_ref[pl.ds(pl.multiple_of(j * t, t), t), :]
        return jnp.dot(w.astype(BF16), vb, preferred_element_type=F32)

    jp = jnp.maximum(qi - 1, 0)
    zd = [scores(hh, qi) for hh in range(2)]
    zp = [scores(hh, jp) for hh in range(2)]
    dd = [decay(z, strictly_causal) for z in zd]
    dp = [decay(z, None) for z in zp]
    pvd = [weigh(zd[hh], dd[hh][0], dd[hh][1], 0.0, strictly_causal, qi) for hh in range(2)]
    pvp = [weigh(zp[hh], dp[hh][0], dp[hh][1], dd[hh][2], None, jp) for hh in range(2)]
    has_prev = qi > 0
    carries = []
    for hh in range(2):
        acc_ref[hh] = pvd[hh] + jnp.where(has_prev, pvp[hh], 0.0)
        carries.append(dd[hh][2] + dp[hh][2])

    def alive(c0, c1):
        return (jnp.minimum(jnp.min(c0), jnp.min(c1)) < EXP_DEAD).astype(jnp.int32)

    def cond(state):
        j, live, _, _ = state
        return jnp.logical_and(j >= 0, live > 0)

    def body(state):
        j, _, c0, c1 = state
        cs = (c0, c1)
        zs = [scores(hh, j) for hh in range(2)]
        ds = [decay(z, None) for z in zs]
        for hh in range(2):
            acc_ref[hh] += weigh(zs[hh], ds[hh][0], ds[hh][1], cs[hh], None, j)
        return j - 1, alive(c0 + ds[0][2], c1 + ds[1][2]), c0 + ds[0][2], c1 + ds[1][2]

    lax.while_loop(cond, body, (qi - 2, alive(*carries), carries[0], carries[1]))
    o_ref[...] = jnp.where(lane < HEAD_DIM, acc_ref[0], acc_ref[1]).astype(o_ref.dtype)


def _sb_attention(qkv, tri, batch, q_col, k_col, v_col):
    m = qkv.shape[0]
    seq = m // batch
    t = ATT_TILE
    nq = seq // t
    n_pairs = N_SB_HEADS * HEAD_DIM // LANES
    return pl.pallas_call(
        _sb_kernel,
        grid=(batch, n_pairs, nq),
        in_specs=[
            pl.BlockSpec((t, LANES), lambda b, p, i: (b * nq + i, q_col + p)),
            pl.BlockSpec((seq, LANES), lambda b, p, i: (b, k_col + p)),
            pl.BlockSpec((seq, LANES), lambda b, p, i: (b, v_col + p)),
            _resident(tri.shape, lambda b, p, i: (0, 0)),
        ],
        out_specs=pl.BlockSpec((t, LANES), lambda b, p, i: (b * nq + i, p)),
        out_shape=jax.ShapeDtypeStruct((m, n_pairs * LANES), BF16),
        scratch_shapes=[pltpu.VMEM((2, t, LANES), F32)],
        compiler_params=_params("parallel", "parallel", "arbitrary"),
        name="sb_attn",
    )(qkv, qkv, qkv, tri)


def _fox_kernel(qt_ref, k_ref, vt_ref, cend_ref, o_ref, y_ref, p_ref, acc_ref, kmax_ref):
    t = ATT_TILE
    qi = pl.program_id(2)
    sub = lax.broadcasted_iota(jnp.int32, (LANES, t), 0)
    own = (sub < HEAD_DIM, sub >= HEAD_DIM)

    @pl.when(qi == 0)
    def _():
        kf = k_ref[:, 0:LANES].astype(F32)
        k2 = kf * kf
        first = lax.broadcasted_iota(jnp.int32, k2.shape, 1) < HEAD_DIM
        for hh, sel in enumerate((first, jnp.logical_not(first))):
            n2 = jnp.sum(jnp.where(sel, k2, 0.0), axis=1, keepdims=True)
            kmax_ref[hh] = jnp.broadcast_to(jnp.sqrt(jnp.max(n2, axis=0, keepdims=True)),
                                            kmax_ref.shape[1:])

    n_sub = qt_ref.shape[0]
    chains = [(s, hh) for s in range(n_sub) for hh in range(2)]
    n_ch = len(chains)
    diag = [n_sub * qi + s for s in range(n_sub)]
    minus_c = jnp.where(sub < 3, -1.0, 0.0).astype(BF16)
    q_aug, z_bound = [], []
    for s, hh in chains:
        q_pair = qt_ref[s]
        qf = q_pair.astype(F32)
        z_bound.append(jnp.sqrt(jnp.sum(jnp.where(own[hh], qf * qf, 0.0), axis=0, keepdims=True))
                       * (kmax_ref[hh][0:1, 0:1] * NORM_SLACK))
        q_aug.append(jnp.concatenate([jnp.where(own[hh], q_pair, jnp.zeros_like(q_pair)), minus_c],
                                     axis=0))
    one = jnp.ones((LANES, t), BF16)

    def logits(c, j):
        kb = k_ref[pl.ds(pl.multiple_of(j * t, t), t), :]
        lhs = kb[:, :2 * LANES] if chains[c][1] == 0 else jnp.concatenate(
            [kb[:, :LANES], kb[:, 2 * LANES:]], axis=1)
        return jnp.dot(lhs, q_aug[c], preferred_element_type=F32)

    def softmax_step(y, m_prev):
        m_new = jnp.maximum(m_prev, jnp.max(y, axis=0, keepdims=True))
        return jnp.exp(y - m_new).astype(BF16), jnp.exp(m_prev - m_new), m_new

    def weighted_values(c, j):
        vt = jnp.where(own[chains[c][1]], vt_ref[j], one)
        return jnp.dot(vt, p_ref[c], preferred_element_type=F32)

    def alive(r_next, ms):
        done = []
        for c, (s, hh) in enumerate(chains):
            j = diag[s] - r_next
            far = cend_ref[pl.ds(jnp.maximum(j, 0), 1), :][:, hh:hh + 1]
            dead = jnp.max(z_bound[c] - far - ms[c]) < -EXP_DEAD
            done.append(jnp.logical_or(j < 0, dead))
        all_done = done[0]
        for d in done[1:]:
            all_done = jnp.logical_and(all_done, d)
        return jnp.logical_not(all_done).astype(jnp.int32)

    key = lax.broadcasted_iota(jnp.int32, (t, t), 0)
    query = lax.broadcasted_iota(jnp.int32, (t, t), 1)
    causal = key <= query
    yd = [logits(c, diag[s]) for c, (s, hh) in enumerate(chains)]
    yn = [logits(c, jnp.maximum(diag[s] - 1, 0)) for c, (s, hh) in enumerate(chains)]
    ms, alphas = [], []
    for c in range(n_ch):
        p, _, m_new = softmax_step(jnp.where(causal, yd[c], MASKED), jnp.full((1, t), MASKED, F32))
        p_ref[c] = p
        y_ref[c] = yn[c]
        acc_ref[c] = jnp.zeros((LANES, t), F32)
        ms.append(m_new)
        alphas.append(jnp.ones((1, t), F32))

    def cond(state):
        r, live = state[0], state[1]
        return jnp.logical_and(r <= diag[n_sub - 1], live > 0)

    def body(state):
        r = state[0]
        ms, alphas = state[2:2 + n_ch], state[2 + n_ch:]
        live = alive(r + 1, ms)
        pv = [weighted_values(c, jnp.maximum(diag[s] - r + 1, 0)) for c, (s, hh) in enumerate(chains)]
        yn = [logits(c, jnp.maximum(diag[s] - r - 1, 0)) for c, (s, hh) in enumerate(chains)]
        new_m, new_a = [], []
        for c, (s, hh) in enumerate(chains):
            y = y_ref[c]
            if s < n_sub - 1:
                y = y + jnp.where(diag[s] - r < 0, MASKED, 0.0)
            p, alpha, m_new = softmax_step(y, ms[c])
            acc_ref[c] = alphas[c] * acc_ref[c] + pv[c]
            p_ref[c] = p
            y_ref[c] = yn[c]
            new_m.append(m_new)
            new_a.append(alpha)
        return (r + 1, live) + tuple(new_m) + tuple(new_a)

    state = lax.while_loop(cond, body, (jnp.int32(1), alive(1, ms)) + tuple(ms) + tuple(alphas))
    r_end = state[0]
    for s in range(n_sub):
        outs = []
        for hh in range(2):
            c = 2 * s + hh
            acc = state[2 + n_ch + c] * acc_ref[c] + weighted_values(
                c, jnp.maximum(diag[s] - r_end + 1, 0))
            outs.append(acc / pltpu.roll(acc, HEAD_DIM, axis=0))
        o_ref[s * t:(s + 1) * t, :] = jnp.where(own[0], outs[0], outs[1]).T.astype(o_ref.dtype)


def _fox_attention(rows, cols, c_ends, batch):
    m = rows.shape[0]
    seq = m // batch
    t = ATT_TILE
    nq = seq // t
    n_pairs = N_FOX_HEADS * HEAD_DIM // LANES
    n_sub = FOX_SUB_TILES
    steps = nq // n_sub
    n_ch = 2 * n_sub
    return pl.pallas_call(
        _fox_kernel,
        grid=(batch, n_pairs, steps),
        in_specs=[
            pl.BlockSpec((None, n_sub, LANES, t), lambda b, p, i: (b, i, p, 0)),
            pl.BlockSpec((seq, 3 * LANES), lambda b, p, i: (b, p)),
            pl.BlockSpec((None, nq, LANES, t), lambda b, p, i: (b, 0, n_pairs + p, 0)),
            pl.BlockSpec((None, None, nq, 2), lambda b, p, i: (b, p, 0, 0)),
        ],
        out_specs=pl.BlockSpec((n_sub * t, LANES), lambda b, p, i: (b * steps + i, p)),
        out_shape=jax.ShapeDtypeStruct((m, n_pairs * LANES), BF16),
        scratch_shapes=[
            pltpu.VMEM((n_ch, t, t), F32),
            pltpu.VMEM((n_ch, t, t), BF16),
            pltpu.VMEM((n_ch, LANES, t), F32),
            pltpu.VMEM((2, 8, LANES), F32),
        ],
        compiler_params=_params("parallel", "parallel", "arbitrary"),
        name="fox_attn",
    )(cols, rows, cols, c_ends)


def _memkv_kernel(mem_ref, g_ref, w_ref, o_ref):
    mem_n = _rms(mem_ref[...], g_ref[...]).astype(BF16)
    o_ref[...] = jnp.dot(mem_n, w_ref[...], preferred_element_type=F32).astype(BF16)


def _memkv(mem2d, g, w_mem_kv):
    n_layers, d, n_out = w_mem_kv.shape
    rows = mem2d.shape[0]
    return pl.pallas_call(
        _memkv_kernel,
        grid=(n_layers,),
        in_specs=[
            _resident((rows, d), lambda l: (0, 0)),
            _resident((1, d), lambda l: (0, 0)),
            pl.BlockSpec((None, d, n_out), lambda l: (l, 0, 0)),
        ],
        out_specs=pl.BlockSpec((None, rows, n_out), lambda l: (l, 0, 0)),
        out_shape=jax.ShapeDtypeStruct((n_layers, rows, n_out), BF16),
        compiler_params=_params("parallel"),
        name="memkv",
    )(mem2d, g, w_mem_kv)


def _post_kernel(l_ref, h_ref, qmem_ref, osb_ref, ofx_ref, kv_ref, gpre_ref, gpost_ref,
                 wgate_ref, bgate_ref, wsb_ref, wfx_ref, wmem_ref, wout_ref, o_ref):
    x = h_ref[...]
    d = x.shape[1]
    u = _rms(x, gpre_ref[...]).astype(BF16)
    gates = jax.nn.sigmoid(jnp.dot(u, wgate_ref[...], preferred_element_type=F32) + bgate_ref[...])

    qm = qmem_ref[...]
    kv = kv_ref[...]
    mem_w = N_MEM_HEADS * MEM_HEAD_DIM
    scale = MEM_HEAD_DIM ** -0.5
    heads = []
    for hh in range(N_MEM_HEADS):
        lo, hi = hh * MEM_HEAD_DIM, (hh + 1) * MEM_HEAD_DIM
        z = lax.dot_general(qm[:, lo:hi], kv[:, lo:hi], _NT, preferred_element_type=F32) * scale
        e = jnp.exp(z - jnp.max(z, axis=1, keepdims=True))
        p = (e / jnp.sum(e, axis=1, keepdims=True)).astype(BF16)
        heads.append(jnp.dot(p, kv[:, mem_w + lo:mem_w + hi], preferred_element_type=F32))
    o_mem = jnp.concatenate(heads, axis=1).astype(BF16)

    merged = (gates[:, 0:d] * jnp.dot(osb_ref[...], wsb_ref[...], preferred_element_type=F32)
              + gates[:, d:2 * d] * jnp.dot(ofx_ref[...], wfx_ref[...], preferred_element_type=F32)
              + gates[:, 2 * d:3 * d] * jnp.dot(o_mem, wmem_ref[...], preferred_element_type=F32))
    y = jnp.dot(merged.astype(BF16), wout_ref[...], preferred_element_type=F32)
    o_ref[...] = x + _rms(y, gpost_ref[...])


def _post(l, h, qkv, o_sb, o_fx, kvmem, g_pre, g_post, w_gate, b_gate, w_sb, w_fx, w_mem, w_out,
          batch, qmem_col):
    m, d = h.shape
    seq = m // batch
    tm = ROW_TILE
    nt = seq // tm
    mem_len = kvmem.shape[1] // batch
    mem_w = N_MEM_HEADS * MEM_HEAD_DIM
    wmap = lambda b, i, l: (l[0], 0, 0)
    rows = lambda b, i, l: (b * nt + i, 0)
    return pl.pallas_call(
        _post_kernel,
        grid_spec=pltpu.PrefetchScalarGridSpec(
            num_scalar_prefetch=1,
            grid=(batch, nt),
            in_specs=[
                pl.BlockSpec((tm, d), rows),
                pl.BlockSpec((tm, mem_w), lambda b, i, l: (b * nt + i, qmem_col)),
                pl.BlockSpec((tm, o_sb.shape[1]), rows),
                pl.BlockSpec((tm, o_fx.shape[1]), rows),
                pl.BlockSpec((None, mem_len, 2 * mem_w), lambda b, i, l: (l[0], b, 0)),
                _resident((None, 1, d), wmap),
                _resident((None, 1, d), wmap),
                _resident((None, d, N_BRANCH * d), wmap),
                _resident((None, 1, N_BRANCH * d), wmap),
                _resident((None,) + w_sb.shape[1:], wmap),
                _resident((None,) + w_fx.shape[1:], wmap),
                _resident((None,) + w_mem.shape[1:], wmap),
                _resident((None, d, d), wmap),
            ],
            out_specs=pl.BlockSpec((tm, d), rows),
        ),
        out_shape=jax.ShapeDtypeStruct((m, d), F32),
        compiler_params=_params("parallel", "parallel"),
        name="post",
    )(l, h, qkv, o_sb, o_fx, kvmem, g_pre, g_post, w_gate, b_gate, w_sb, w_fx, w_mem, w_out)


def kernel(x, mem, ffn1_pre_g, ffn1_post_g, ffn1_w_gate, ffn1_w_up, ffn1_w_down, mix_pre_g, mix_post_g, w_in, b_forget, mem_norm_g, w_mem_kv, w_gate, b_gate, w_br_sb, w_br_fox, w_br_mem, w_out, ffn2_pre_g, ffn2_post_g, ffn2_w_gate, ffn2_w_up, ffn2_w_down):
    batch, seq, d = x.shape
    n_layers = w_in.shape[0]
    sb_w = N_SB_HEADS * HEAD_DIM
    fox_w = N_FOX_HEADS * HEAD_DIM
    mem_w = N_MEM_HEADS * MEM_HEAD_DIM
    assert seq % ROW_TILE == 0 and seq % ATT_TILE == 0 and d % LANES == 0
    assert w_in.shape[2] == 3 * sb_w + 3 * fox_w + N_FOX_HEADS + mem_w

    scale = HEAD_DIM ** -0.5
    fx = 3 * sb_w
    f_lo = fx + 3 * fox_w
    f_hi = f_lo + N_FOX_HEADS
    bf = lambda w: w.astype(BF16)
    vec = lambda g: g[:, None, :]

    order = jnp.argsort(b_forget, axis=1)

    def by_head(w, axis):
        shape = [1] * w.ndim
        shape[0], shape[axis] = order.shape[0], order.shape[1]
        return jnp.take_along_axis(w, order.reshape(shape), axis=axis)

    def fox_cols(lo):
        w = w_in[:, :, lo:lo + fox_w].reshape(n_layers, d, N_FOX_HEADS, HEAD_DIM)
        return by_head(w, 2).reshape(n_layers, d, fox_w)

    w_gate_f = by_head(w_in[:, :, f_lo:f_hi], 2)
    g_hi = w_gate_f.astype(BF16)
    g_mid = (w_gate_f - g_hi.astype(F32)).astype(BF16)
    g_lo = ((w_gate_f - g_hi.astype(F32)) - g_mid.astype(F32)).astype(BF16)
    gate3 = jnp.pad(jnp.concatenate([g_hi, g_mid, g_lo], axis=2),
                    ((0, 0), (0, 0), (0, LANES - 3 * N_FOX_HEADS)))
    w_main = jnp.concatenate([bf(fox_cols(fx + fox_w)), bf(w_in[:, :, :sb_w] * scale),
                              bf(w_in[:, :, sb_w:fx]), bf(w_in[:, :, f_hi:])], axis=2)
    w_t = jnp.swapaxes(jnp.concatenate([fox_cols(fx) * scale, fox_cols(fx + 2 * fox_w)], axis=2),
                       1, 2).astype(BF16)
    b_f = jnp.pad(by_head(b_forget, 1), ((0, 0), (0, LANES - N_FOX_HEADS)))[:, None, :]
    w_br_fox_sorted = by_head(w_br_fox.reshape(n_layers, N_FOX_HEADS, HEAD_DIM, d), 1).reshape(
        w_br_fox.shape)
    term, head = jnp.meshgrid(jnp.arange(3), jnp.arange(N_FOX_HEADS), indexing="ij")
    place = jnp.zeros((LANES, N_FOX_HEADS * LANES), BF16).at[
        term * N_FOX_HEADS + head, head * LANES + term].set(1.0)
    tok = jnp.arange(ROW_TILE)
    lower = (tok[:, None] >= tok[None, :]).astype(BF16)
    weights = dict(
        f1=(vec(ffn1_pre_g), vec(ffn1_post_g), bf(ffn1_w_gate), bf(ffn1_w_up), bf(ffn1_w_down)),
        f2=(vec(ffn2_pre_g), vec(ffn2_post_g), bf(ffn2_w_gate), bf(ffn2_w_up), bf(ffn2_w_down)),
        proj=(vec(mix_pre_g), w_main, w_t, gate3, b_f, lower, place),
        post=(vec(mix_pre_g), vec(mix_post_g), bf(w_gate), vec(b_gate), bf(w_br_sb),
              bf(w_br_fox_sorted), bf(w_br_mem), bf(w_out)),
    )

    t = ATT_TILE
    later = (jnp.arange(t)[:, None] > jnp.arange(t)[None, :]).astype(BF16)
    tri = jnp.concatenate([later, later], axis=0)

    kvmem = _memkv(mem.reshape(batch * mem.shape[1], d), mem_norm_g[None, :], bf(w_mem_kv))

    fox_k_w = (fox_w // LANES) * 3 * LANES
    q_sb_col = fox_k_w // LANES
    k_sb_col, v_sb_col = q_sb_col + sb_w // LANES, q_sb_col + 2 * sb_w // LANES
    qmem_col = (fox_k_w + fx) // mem_w
    nq = seq // t

    def layer(h, l):
        l = l[None]
        h = _ffn(l, h, *weights["f1"])
        rows, cols, c = _proj(l, h, *weights["proj"], batch)
        o_sb = _sb_attention(rows, tri, batch, q_sb_col, k_sb_col, v_sb_col)
        c_ends = c.reshape(batch, nq, t, LANES)[:, :, t - 1, :N_FOX_HEADS]
        c_ends = c_ends.reshape(batch, nq, N_FOX_HEADS // 2, 2).transpose(0, 2, 1, 3)
        o_fx = _fox_attention(rows, cols, c_ends, batch)
        h = _post(l, h, rows, o_sb, o_fx, kvmem, *weights["post"], batch, qmem_col)
        h = _ffn(l, h, *weights["f2"])
        return h, None

    h, _ = lax.scan(layer, x.reshape(batch * seq, d), jnp.arange(n_layers, dtype=jnp.int32))
    return h.reshape(batch, seq, d)
```

```python
import math

import jax
import jax.numpy as jnp
from jax import lax
from jax.experimental import pallas as pl
from jax.experimental.pallas import tpu as pltpu

F32 = jnp.float32
BF16 = jnp.bfloat16

HEAD_DIM = 64
N_SB_HEADS = 8
N_FOX_HEADS = 8
N_MEM_HEADS = 4
MEM_HEAD_DIM = 128
N_BRANCH = 3
RMS_EPS = 1e-6
LOG2_E = math.log2(math.e)

LANES = 128
VMEM_LIMIT_BYTES = 56 * 1024 * 1024
ROW_TILE = 512
ATT_TILE = 256
FOX_SUB_TILES = 2
SB_SUB_TILES = 2
KNORM_LANE = 32
EXP_DEAD = 104.0
EXP2_DEAD = 151.0
SOFTPLUS2_CLAMP = 86.0
NORM_SLACK = 1.01
MASKED = -1e30

_NT = (((1,), (1,)), ((), ()))


def _rms(x, g):
    return x * lax.rsqrt(jnp.mean(x * x, axis=-1, keepdims=True) + RMS_EPS) * g


def _params(*sem):
    return pltpu.CompilerParams(dimension_semantics=sem, vmem_limit_bytes=VMEM_LIMIT_BYTES)


def _resident(shape, index_map):
    return pl.BlockSpec(shape, index_map, pipeline_mode=pl.Buffered(1))


def _ffn_kernel(l_ref, h_ref, gpre_ref, gpost_ref, wg_ref, wu_ref, wd_ref, o_ref):
    x = h_ref[...]
    u = _rms(x, gpre_ref[...]).astype(BF16)
    g = jnp.dot(u, wg_ref[...], preferred_element_type=F32)
    up = jnp.dot(u, wu_ref[...], preferred_element_type=F32)
    a = (g * jax.nn.sigmoid(g) * up).astype(BF16)
    f = jnp.dot(a, wd_ref[...], preferred_element_type=F32)
    o_ref[...] = x + 0.5 * _rms(f, gpost_ref[...])


def _ffn(l, h, g_pre, g_post, w_gate, w_up, w_down):
    m, d = h.shape
    f = w_gate.shape[-1]
    tm = ROW_TILE
    wmap = lambda i, l: (l[0], 0, 0)
    return pl.pallas_call(
        _ffn_kernel,
        grid_spec=pltpu.PrefetchScalarGridSpec(
            num_scalar_prefetch=1,
            grid=(m // tm,),
            in_specs=[
                pl.BlockSpec((tm, d), lambda i, l: (i, 0)),
                _resident((None, 1, d), wmap),
                _resident((None, 1, d), wmap),
                _resident((None, d, f), wmap),
                _resident((None, d, f), wmap),
                _resident((None, f, d), wmap),
            ],
            out_specs=pl.BlockSpec((tm, d), lambda i, l: (i, 0)),
        ),
        out_shape=jax.ShapeDtypeStruct((m, d), F32),
        compiler_params=_params("parallel"),
        name="ffn",
    )(l, h, g_pre, g_post, w_gate, w_up, w_down)


def _split3(x):
    hi = x.astype(BF16).astype(F32)
    mid = (x - hi).astype(BF16).astype(F32)
    lo = ((x - hi) - mid).astype(BF16).astype(F32)
    return hi, mid, lo


def _fold3(x, nh):
    n = x.shape[1]
    return x + pltpu.roll(x, n - nh, axis=1) + pltpu.roll(x, n - 2 * nh, axis=1)


def _spread3(terms, nh):
    lane = lax.broadcasted_iota(jnp.int32, terms[0].shape, 1)
    out = jnp.where(lane < nh, terms[0], 0.0)
    for i in (1, 2):
        moved = pltpu.roll(terms[i], i * nh, axis=1)
        out = jnp.where(jnp.logical_and(lane >= i * nh, lane < (i + 1) * nh), moved, out)
    return out.astype(BF16)


def _proj_kernel(l_ref, h_ref, g_ref, w_ref, wt_ref, wg_ref, bf_ref, lower_ref, place_ref,
                 rows_ref, cols_ref, c_ref, carry_ref):
    @pl.when(pl.program_id(1) == 0)
    def _():
        carry_ref[...] = jnp.zeros_like(carry_ref)

    nh = N_FOX_HEADS
    u = _rms(h_ref[...], g_ref[...])
    u_hi, u_mid, u_lo = _split3(u)
    ub = u_hi.astype(BF16)

    w_gate3 = wg_ref[...]
    e_hi = jnp.dot(ub, w_gate3, preferred_element_type=F32)
    e_mid = jnp.dot(u_mid.astype(BF16), w_gate3, preferred_element_type=F32)
    e_lo = jnp.dot(u_lo.astype(BF16), w_gate3, preferred_element_type=F32)
    drop_lo = lax.broadcasted_iota(jnp.int32, e_mid.shape, 1) < 2 * nh
    fl = _fold3(e_hi + jnp.where(drop_lo, e_mid, 0.0), nh) + e_lo + bf_ref[...]
    log_f = jnp.minimum(fl, 0.0) - jnp.log1p(jnp.exp(-jnp.abs(fl)))

    sums = jnp.dot(lower_ref[...], _spread3(_split3(log_f), nh), preferred_element_type=F32)
    c = _fold3(sums, nh) + carry_ref[0:1, :]
    tm = c.shape[0]
    carry_ref[...] = jnp.broadcast_to(c[tm - 1:tm, :], carry_ref.shape)
    placed = jnp.dot(_spread3(_split3(c), nh), place_ref[...], preferred_element_type=F32)

    main = jnp.dot(ub, w_ref[...], preferred_element_type=F32)
    n_k = N_FOX_HEADS * HEAD_DIM
    kf = main[:, :n_k]
    dim = lax.broadcasted_iota(jnp.int32, (n_k, LANES), 0)
    lane_k = lax.broadcasted_iota(jnp.int32, (n_k, LANES), 1)
    head_of_dim = jnp.where(lane_k == KNORM_LANE + dim // HEAD_DIM, 1.0, 0.0).astype(BF16)
    norms2 = jnp.dot((kf * kf).astype(BF16), head_of_dim, preferred_element_type=F32)
    lane_c = lax.broadcasted_iota(jnp.int32, c.shape, 1)
    c_ref[...] = jnp.where(jnp.logical_and(lane_c >= KNORM_LANE, lane_c < KNORM_LANE + nh),
                           norms2, c)
    pair_w = 3 * LANES
    for p in range(n_k // LANES):
        base = p * pair_w
        rows_ref[:, base:base + LANES] = main[:, p * LANES:(p + 1) * LANES].astype(BF16)
        rows_ref[:, base + LANES:base + pair_w] = (
            placed[:, 2 * p * LANES:2 * (p + 1) * LANES].astype(BF16))
    rows_ref[:, (n_k // LANES) * pair_w:] = main[:, n_k:].astype(BF16)

    across = lax.dot_general(wt_ref[...], ub, _NT, preferred_element_type=F32).astype(BF16)
    t = ATT_TILE
    for s in range(cols_ref.shape[0]):
        cols_ref[s] = across[:, s * t:(s + 1) * t]


def _proj(l, h, g, w_main, w_t, w_gate3, b_f, lower, place, batch):
    m, d = h.shape
    seq = m // batch
    tm = ROW_TILE
    nt = seq // tm
    per = tm // ATT_TILE
    n_k = N_FOX_HEADS * HEAD_DIM
    n_rows_out = w_main.shape[-1] - n_k + (n_k // LANES) * 3 * LANES
    n_cols_out = w_t.shape[1]
    wmap = lambda b, i, l: (l[0], 0, 0)
    const = lambda b, i, l: (0, 0)
    return pl.pallas_call(
        _proj_kernel,
        grid_spec=pltpu.PrefetchScalarGridSpec(
            num_scalar_prefetch=1,
            grid=(batch, nt),
            in_specs=[
                pl.BlockSpec((tm, d), lambda b, i, l: (b * nt + i, 0)),
                _resident((None, 1, d), wmap),
                _resident((None, d, w_main.shape[-1]), wmap),
                _resident((None, n_cols_out, d), wmap),
                _resident((None, d, LANES), wmap),
                _resident((None, 1, LANES), wmap),
                _resident(lower.shape, const),
                _resident(place.shape, const),
            ],
            out_specs=[
                pl.BlockSpec((tm, n_rows_out), lambda b, i, l: (b * nt + i, 0)),
                pl.BlockSpec((None, per, n_cols_out, ATT_TILE), lambda b, i, l: (b, i, 0, 0)),
                pl.BlockSpec((tm, LANES), lambda b, i, l: (b * nt + i, 0)),
            ],
            scratch_shapes=[pltpu.VMEM((8, LANES), F32)],
        ),
        out_shape=[
            jax.ShapeDtypeStruct((m, n_rows_out), BF16),
            jax.ShapeDtypeStruct((batch, seq // ATT_TILE, n_cols_out, ATT_TILE), BF16),
            jax.ShapeDtypeStruct((m, LANES), F32),
        ],
        compiler_params=_params("parallel", "arbitrary"),
        name="proj",
    )(l, h, g, w_main, w_t, w_gate3, b_f, lower, place)


def _softplus2(z2):
    return jnp.maximum(z2, jnp.log2(1.0 + jnp.exp2(jnp.minimum(z2, SOFTPLUS2_CLAMP))))


def _sb_kernel(q_ref, k_ref, v_ref, tri_ref, o_ref, acc_ref):
    t = ATT_TILE
    qi = pl.program_id(2)
    lane = lax.broadcasted_iota(jnp.int32, (t, LANES), 1)
    own = (lane < HEAD_DIM, lane >= HEAD_DIM)
    n_sub = q_ref.shape[0] // t
    chains = [(s, hh) for s in range(n_sub) for hh in range(2)]
    n_ch = len(chains)
    diag = [n_sub * qi + s for s in range(n_sub)]
    q_heads = []
    for s, hh in chains:
        q = q_ref[s * t:(s + 1) * t, :]
        q_heads.append(jnp.where(own[hh], q, jnp.zeros_like(q)))
    tri = tri_ref[...]

    row = lax.broadcasted_iota(jnp.int32, (t, t), 0)
    col = lax.broadcasted_iota(jnp.int32, (t, t), 1)
    strictly_causal = col < row

    def scores(c, j):
        kb = k_ref[pl.ds(pl.multiple_of(j * t, t), t), :]
        return lax.dot_general(q_heads[c], kb, _NT, preferred_element_type=F32)

    def decay(z, mask):
        sp = _softplus2(z)
        if mask is not None:
            sp = jnp.where(mask, sp, 0.0)
        hi = sp.astype(BF16)
        lo = (sp - hi.astype(F32)).astype(BF16)
        later = jnp.dot(jnp.concatenate([hi, lo], axis=1), tri, preferred_element_type=F32)
        return sp, later, jnp.sum(sp, axis=1, keepdims=True)

    def weigh(z, sp, later, carry, mask, j):
        w = jnp.exp2((z - sp) - later - carry)
        if mask is not None:
            w = jnp.where(mask, w, 0.0)
        vb = v_ref[pl.ds(pl.multiple_of(j * t, t), t), :]
        return jnp.dot(w.astype(BF16), vb, preferred_element_type=F32)

    prev = [jnp.maximum(diag[s] - 1, 0) for s, hh in chains]
    zd = [scores(c, diag[s]) for c, (s, hh) in enumerate(chains)]
    zp = [scores(c, prev[c]) for c in range(n_ch)]
    dd = [decay(z, strictly_causal) for z in zd]
    dp = [decay(z, None) for z in zp]
    pvd = [weigh(zd[c], dd[c][0], dd[c][1], 0.0, strictly_causal, diag[s])
           for c, (s, hh) in enumerate(chains)]
    pvp = [weigh(zp[c], dp[c][0], dp[c][1], dd[c][2], None, prev[c]) for c in range(n_ch)]
    carries = []
    for c, (s, hh) in enumerate(chains):
        acc_ref[c] = pvd[c] + jnp.where(diag[s] > 0, pvp[c], 0.0)
        carries.append(dd[c][2] + dp[c][2])

    def alive(r_next, cs):
        more = [jnp.logical_and(diag[s] - r_next >= 0, jnp.min(cs[c]) < EXP2_DEAD)
                for c, (s, hh) in enumerate(chains)]
        any_more = more[0]
        for m in more[1:]:
            any_more = jnp.logical_or(any_more, m)
        return any_more.astype(jnp.int32)

    def cond(state):
        return state[1] > 0

    def body(state):
        r, cs = state[0], state[2:]
        js = [diag[s] - r for s, hh in chains]
        zs = [scores(c, jnp.maximum(js[c], 0)) for c in range(n_ch)]
        ds = [decay(z, None) for z in zs]
        new = []
        for c in range(n_ch):
            pv = weigh(zs[c], ds[c][0], ds[c][1], cs[c], None, jnp.maximum(js[c], 0))
            acc_ref[c] += jnp.where(js[c] >= 0, pv, 0.0)
            new.append(cs[c] + ds[c][2])
        return (r + 1, alive(r + 1, new)) + tuple(new)

    lax.while_loop(cond, body, (jnp.int32(2), alive(2, carries)) + tuple(carries))
    for s in range(n_sub):
        o_ref[s * t:(s + 1) * t, :] = jnp.where(
            own[0], acc_ref[2 * s], acc_ref[2 * s + 1]).astype(o_ref.dtype)


def _sb_attention(qkv, tri, batch, q_col, k_col, v_col):
    m = qkv.shape[0]
    seq = m // batch
    t = ATT_TILE
    n_sub = SB_SUB_TILES
    steps = seq // (n_sub * t)
    n_pairs = N_SB_HEADS * HEAD_DIM // LANES
    return pl.pallas_call(
        _sb_kernel,
        grid=(batch, n_pairs, steps),
        in_specs=[
            pl.BlockSpec((n_sub * t, LANES), lambda b, p, i: (b * steps + i, q_col + p)),
            pl.BlockSpec((seq, LANES), lambda b, p, i: (b, k_col + p)),
            pl.BlockSpec((seq, LANES), lambda b, p, i: (b, v_col + p)),
            _resident(tri.shape, lambda b, p, i: (0, 0)),
        ],
        out_specs=pl.BlockSpec((n_sub * t, LANES), lambda b, p, i: (b * steps + i, p)),
        out_shape=jax.ShapeDtypeStruct((m, n_pairs * LANES), BF16),
        scratch_shapes=[pltpu.VMEM((2 * n_sub, t, LANES), F32)],
        compiler_params=_params("parallel", "parallel", "arbitrary"),
        name="sb_attn",
    )(qkv, qkv, qkv, tri)


def _fox_kernel(qt_ref, k_ref, vt_ref, cend_ref, o_ref, y_ref, p_ref, acc_ref):
    t = ATT_TILE
    qi = pl.program_id(2)
    sub = lax.broadcasted_iota(jnp.int32, (LANES, t), 0)
    own = (sub < HEAD_DIM, sub >= HEAD_DIM)
    n_tiles = cend_ref.shape[0] - 1
    k_max = cend_ref[n_tiles:n_tiles + 1, :]

    n_sub = qt_ref.shape[0]
    chains = [(s, hh) for s in range(n_sub) for hh in range(2)]
    n_ch = len(chains)
    diag = [n_sub * qi + s for s in range(n_sub)]
    minus_c = jnp.where(sub < 3, -1.0, 0.0).astype(BF16)
    q_aug, z_bound = [], []
    for s, hh in chains:
        q_pair = qt_ref[s]
        qf = q_pair.astype(F32)
        z_bound.append(jnp.sqrt(jnp.sum(jnp.where(own[hh], qf * qf, 0.0), axis=0, keepdims=True))
                       * (k_max[:, hh:hh + 1] * NORM_SLACK))
        q_aug.append(jnp.concatenate([jnp.where(own[hh], q_pair, jnp.zeros_like(q_pair)), minus_c],
                                     axis=0))
    one = jnp.ones((LANES, t), BF16)

    def logits(c, j):
        kb = k_ref[pl.ds(pl.multiple_of(j * t, t), t), :]
        lhs = kb[:, :2 * LANES] if chains[c][1] == 0 else jnp.concatenate(
            [kb[:, :LANES], kb[:, 2 * LANES:]], axis=1)
        return jnp.dot(lhs, q_aug[c], preferred_element_type=F32)

    def softmax_step(y, m_prev):
        m_new = jnp.maximum(m_prev, jnp.max(y, axis=0, keepdims=True))
        return jnp.exp(y - m_new).astype(BF16), jnp.exp(m_prev - m_new), m_new

    def weighted_values(c, j):
        vt = jnp.where(own[chains[c][1]], vt_ref[j], one)
        return jnp.dot(vt, p_ref[c], preferred_element_type=F32)

    def alive(r_next, ms):
        done = []
        for c, (s, hh) in enumerate(chains):
            j = diag[s] - r_next
            far = cend_ref[pl.ds(jnp.maximum(j, 0), 1), :][:, hh:hh + 1]
            dead = jnp.max(z_bound[c] - far - ms[c]) < -EXP_DEAD
            done.append(jnp.logical_or(j < 0, dead))
        all_done = done[0]
        for d in done[1:]:
            all_done = jnp.logical_and(all_done, d)
        return jnp.logical_not(all_done).astype(jnp.int32)

    key = lax.broadcasted_iota(jnp.int32, (t, t), 0)
    query = lax.broadcasted_iota(jnp.int32, (t, t), 1)
    causal = key <= query
    yd = [logits(c, diag[s]) for c, (s, hh) in enumerate(chains)]
    yn = [logits(c, jnp.maximum(diag[s] - 1, 0)) for c, (s, hh) in enumerate(chains)]
    ms, alphas = [], []
    for c in range(n_ch):
        p, _, m_new = softmax_step(jnp.where(causal, yd[c], MASKED), jnp.full((1, t), MASKED, F32))
        p_ref[c] = p
        y_ref[c] = yn[c]
        acc_ref[c] = jnp.zeros((LANES, t), F32)
        ms.append(m_new)
        alphas.append(jnp.ones((1, t), F32))

    def cond(state):
        r, live = state[0], state[1]
        return jnp.logical_and(r <= diag[n_sub - 1], live > 0)

    def body(state):
        r = state[0]
        ms, alphas = state[2:2 + n_ch], state[2 + n_ch:]
        live = alive(r + 1, ms)
        pv = [weighted_values(c, jnp.maximum(diag[s] - r + 1, 0)) for c, (s, hh) in enumerate(chains)]
        yn = [logits(c, jnp.maximum(diag[s] - r - 1, 0)) for c, (s, hh) in enumerate(chains)]
        new_m, new_a = [], []
        for c, (s, hh) in enumerate(chains):
            y = y_ref[c]
            if s < n_sub - 1:
                y = y + jnp.where(diag[s] - r < 0, MASKED, 0.0)
            p, alpha, m_new = softmax_step(y, ms[c])
            acc_ref[c] = alphas[c] * acc_ref[c] + pv[c]
            p_ref[c] = p
            y_ref[c] = yn[c]
            new_m.append(m_new)
            new_a.append(alpha)
        return (r + 1, live) + tuple(new_m) + tuple(new_a)

    state = lax.while_loop(cond, body, (jnp.int32(1), alive(1, ms)) + tuple(ms) + tuple(alphas))
    r_end = state[0]
    for s in range(n_sub):
        outs = []
        for hh in range(2):
            c = 2 * s + hh
            acc = state[2 + n_ch + c] * acc_ref[c] + weighted_values(
                c, jnp.maximum(diag[s] - r_end + 1, 0))
            outs.append(acc / pltpu.roll(acc, HEAD_DIM, axis=0))
        o_ref[s * t:(s + 1) * t, :] = jnp.where(own[0], outs[0], outs[1]).T.astype(o_ref.dtype)


def _fox_attention(rows, cols, c_ends, batch):
    m = rows.shape[0]
    seq = m // batch
    t = ATT_TILE
    nq = seq // t
    n_pairs = N_FOX_HEADS * HEAD_DIM // LANES
    n_sub = FOX_SUB_TILES
    steps = nq // n_sub
    n_ch = 2 * n_sub
    return pl.pallas_call(
        _fox_kernel,
        grid=(batch, n_pairs, steps),
        in_specs=[
            pl.BlockSpec((None, n_sub, LANES, t), lambda b, p, i: (b, i, p, 0)),
            pl.BlockSpec((seq, 3 * LANES), lambda b, p, i: (b, p)),
            pl.BlockSpec((None, nq, LANES, t), lambda b, p, i: (b, 0, n_pairs + p, 0)),
            pl.BlockSpec((None, None, nq + 1, 2), lambda b, p, i: (b, p, 0, 0)),
        ],
        out_specs=pl.BlockSpec((n_sub * t, LANES), lambda b, p, i: (b * steps + i, p)),
        out_shape=jax.ShapeDtypeStruct((m, n_pairs * LANES), BF16),
        scratch_shapes=[
            pltpu.VMEM((n_ch, t, t), F32),
            pltpu.VMEM((n_ch, t, t), BF16),
            pltpu.VMEM((n_ch, LANES, t), F32),
        ],
        compiler_params=_params("parallel", "parallel", "arbitrary"),
        name="fox_attn",
    )(cols, rows, cols, c_ends)


def _memkv_kernel(mem_ref, g_ref, w_ref, o_ref):
    mem_n = _rms(mem_ref[...], g_ref[...]).astype(BF16)
    o_ref[...] = jnp.dot(mem_n, w_ref[...], preferred_element_type=F32).astype(BF16)


def _memkv(mem2d, g, w_mem_kv):
    n_layers, d, n_out = w_mem_kv.shape
    rows = mem2d.shape[0]
    return pl.pallas_call(
        _memkv_kernel,
        grid=(n_layers,),
        in_specs=[
            _resident((rows, d), lambda l: (0, 0)),
            _resident((1, d), lambda l: (0, 0)),
            pl.BlockSpec((None, d, n_out), lambda l: (l, 0, 0)),
        ],
        out_specs=pl.BlockSpec((None, rows, n_out), lambda l: (l, 0, 0)),
        out_shape=jax.ShapeDtypeStruct((n_layers, rows, n_out), BF16),
        compiler_params=_params("parallel"),
        name="memkv",
    )(mem2d, g, w_mem_kv)


def _post_kernel(l_ref, h_ref, qmem_ref, osb_ref, ofx_ref, kv_ref, gpre_ref, gpost_ref,
                 wgate_ref, bgate_ref, wsb_ref, wfx_ref, wmem_ref, wout_ref, o_ref):
    x = h_ref[...]
    d = x.shape[1]
    u = _rms(x, gpre_ref[...]).astype(BF16)
    gates = jax.nn.sigmoid(jnp.dot(u, wgate_ref[...], preferred_element_type=F32) + bgate_ref[...])

    qm = qmem_ref[...]
    kv = kv_ref[...]
    mem_w = N_MEM_HEADS * MEM_HEAD_DIM
    scale = MEM_HEAD_DIM ** -0.5
    heads = []
    for hh in range(N_MEM_HEADS):
        lo, hi = hh * MEM_HEAD_DIM, (hh + 1) * MEM_HEAD_DIM
        z = lax.dot_general(qm[:, lo:hi], kv[:, lo:hi], _NT, preferred_element_type=F32) * scale
        e = jnp.exp(z - jnp.max(z, axis=1, keepdims=True))
        p = (e / jnp.sum(e, axis=1, keepdims=True)).astype(BF16)
        heads.append(jnp.dot(p, kv[:, mem_w + lo:mem_w + hi], preferred_element_type=F32))
    o_mem = jnp.concatenate(heads, axis=1).astype(BF16)

    merged = (gates[:, 0:d] * jnp.dot(osb_ref[...], wsb_ref[...], preferred_element_type=F32)
              + gates[:, d:2 * d] * jnp.dot(ofx_ref[...], wfx_ref[...], preferred_element_type=F32)
              + gates[:, 2 * d:3 * d] * jnp.dot(o_mem, wmem_ref[...], preferred_element_type=F32))
    y = jnp.dot(merged.astype(BF16), wout_ref[...], preferred_element_type=F32)
    o_ref[...] = x + _rms(y, gpost_ref[...])


def _post(l, h, qkv, o_sb, o_fx, kvmem, g_pre, g_post, w_gate, b_gate, w_sb, w_fx, w_mem, w_out,
          batch, qmem_col):
    m, d = h.shape
    seq = m // batch
    tm = ROW_TILE
    nt = seq // tm
    mem_len = kvmem.shape[1] // batch
    mem_w = N_MEM_HEADS * MEM_HEAD_DIM
    wmap = lambda b, i, l: (l[0], 0, 0)
    rows = lambda b, i, l: (b * nt + i, 0)
    return pl.pallas_call(
        _post_kernel,
        grid_spec=pltpu.PrefetchScalarGridSpec(
            num_scalar_prefetch=1,
            grid=(batch, nt),
            in_specs=[
                pl.BlockSpec((tm, d), rows),
                pl.BlockSpec((tm, mem_w), lambda b, i, l: (b * nt + i, qmem_col)),
                pl.BlockSpec((tm, o_sb.shape[1]), rows),
                pl.BlockSpec((tm, o_fx.shape[1]), rows),
                pl.BlockSpec((None, mem_len, 2 * mem_w), lambda b, i, l: (l[0], b, 0)),
                _resident((None, 1, d), wmap),
                _resident((None, 1, d), wmap),
                _resident((None, d, N_BRANCH * d), wmap),
                _resident((None, 1, N_BRANCH * d), wmap),
                _resident((None,) + w_sb.shape[1:], wmap),
                _resident((None,) + w_fx.shape[1:], wmap),
                _resident((None,) + w_mem.shape[1:], wmap),
                _resident((None, d, d), wmap),
            ],
            out_specs=pl.BlockSpec((tm, d), rows),
        ),
        out_shape=jax.ShapeDtypeStruct((m, d), F32),
        compiler_params=_params("parallel", "parallel"),
        name="post",
    )(l, h, qkv, o_sb, o_fx, kvmem, g_pre, g_post, w_gate, b_gate, w_sb, w_fx, w_mem, w_out)


def kernel(x, mem, ffn1_pre_g, ffn1_post_g, ffn1_w_gate, ffn1_w_up, ffn1_w_down, mix_pre_g, mix_post_g, w_in, b_forget, mem_norm_g, w_mem_kv, w_gate, b_gate, w_br_sb, w_br_fox, w_br_mem, w_out, ffn2_pre_g, ffn2_post_g, ffn2_w_gate, ffn2_w_up, ffn2_w_down):
    batch, seq, d = x.shape
    n_layers = w_in.shape[0]
    sb_w = N_SB_HEADS * HEAD_DIM
    fox_w = N_FOX_HEADS * HEAD_DIM
    mem_w = N_MEM_HEADS * MEM_HEAD_DIM
    assert seq % ROW_TILE == 0 and seq % ATT_TILE == 0 and d % LANES == 0
    assert w_in.shape[2] == 3 * sb_w + 3 * fox_w + N_FOX_HEADS + mem_w

    scale = HEAD_DIM ** -0.5
    fx = 3 * sb_w
    f_lo = fx + 3 * fox_w
    f_hi = f_lo + N_FOX_HEADS
    bf = lambda w: w.astype(BF16)
    vec = lambda g: g[:, None, :]

    order = jnp.argsort(b_forget, axis=1)

    def by_head(w, axis):
        shape = [1] * w.ndim
        shape[0], shape[axis] = order.shape[0], order.shape[1]
        return jnp.take_along_axis(w, order.reshape(shape), axis=axis)

    def fox_cols(lo):
        w = w_in[:, :, lo:lo + fox_w].reshape(n_layers, d, N_FOX_HEADS, HEAD_DIM)
        return by_head(w, 2).reshape(n_layers, d, fox_w)

    w_gate_f = by_head(w_in[:, :, f_lo:f_hi], 2)
    g_hi = w_gate_f.astype(BF16)
    g_mid = (w_gate_f - g_hi.astype(F32)).astype(BF16)
    g_lo = ((w_gate_f - g_hi.astype(F32)) - g_mid.astype(F32)).astype(BF16)
    gate3 = jnp.pad(jnp.concatenate([g_hi, g_mid, g_lo], axis=2),
                    ((0, 0), (0, 0), (0, LANES - 3 * N_FOX_HEADS)))
    w_main = jnp.concatenate([bf(fox_cols(fx + fox_w)), bf(w_in[:, :, :sb_w] * (scale * LOG2_E)),
                              bf(w_in[:, :, sb_w:fx]), bf(w_in[:, :, f_hi:])], axis=2)
    w_t = jnp.swapaxes(jnp.concatenate([fox_cols(fx) * scale, fox_cols(fx + 2 * fox_w)], axis=2),
                       1, 2).astype(BF16)
    b_f = jnp.pad(by_head(b_forget, 1), ((0, 0), (0, LANES - N_FOX_HEADS)))[:, None, :]
    w_br_fox_sorted = by_head(w_br_fox.reshape(n_layers, N_FOX_HEADS, HEAD_DIM, d), 1).reshape(
        w_br_fox.shape)
    term, head = jnp.meshgrid(jnp.arange(3), jnp.arange(N_FOX_HEADS), indexing="ij")
    place = jnp.zeros((LANES, N_FOX_HEADS * LANES), BF16).at[
        term * N_FOX_HEADS + head, head * LANES + term].set(1.0)
    tok = jnp.arange(ROW_TILE)
    lower = (tok[:, None] >= tok[None, :]).astype(BF16)
    weights = dict(
        f1=(vec(ffn1_pre_g), vec(ffn1_post_g), bf(ffn1_w_gate), bf(ffn1_w_up), bf(ffn1_w_down)),
        f2=(vec(ffn2_pre_g), vec(ffn2_post_g), bf(ffn2_w_gate), bf(ffn2_w_up), bf(ffn2_w_down)),
        proj=(vec(mix_pre_g), w_main, w_t, gate3, b_f, lower, place),
        post=(vec(mix_pre_g), vec(mix_post_g), bf(w_gate), vec(b_gate), bf(w_br_sb),
              bf(w_br_fox_sorted), bf(w_br_mem), bf(w_out)),
    )

    t = ATT_TILE
    later = (jnp.arange(t)[:, None] > jnp.arange(t)[None, :]).astype(BF16)
    tri = jnp.concatenate([later, later], axis=0)

    kvmem = _memkv(mem.reshape(batch * mem.shape[1], d), mem_norm_g[None, :], bf(w_mem_kv))

    fox_k_w = (fox_w // LANES) * 3 * LANES
    q_sb_col = fox_k_w // LANES
    k_sb_col, v_sb_col = q_sb_col + sb_w // LANES, q_sb_col + 2 * sb_w // LANES
    qmem_col = (fox_k_w + fx) // mem_w
    nq = seq // t

    h = x.reshape(batch * seq, d)
    for layer in range(n_layers):
        l = jnp.full((1,), layer, jnp.int32)
        h = _ffn(l, h, *weights["f1"])
        rows, cols, c = _proj(l, h, *weights["proj"], batch)
        o_sb = _sb_attention(rows, tri, batch, q_sb_col, k_sb_col, v_sb_col)
        c = c.reshape(batch, nq, t, LANES)
        c_ends = c[:, :, t - 1, :N_FOX_HEADS]
        k_max = jnp.sqrt(jnp.max(c[:, :, :, KNORM_LANE:KNORM_LANE + N_FOX_HEADS], axis=(1, 2)))
        c_ends = jnp.concatenate([c_ends, k_max[:, None, :]], axis=1)
        c_ends = c_ends.reshape(batch, nq + 1, N_FOX_HEADS // 2, 2).transpose(0, 2, 1, 3)
        o_fx = _fox_attention(rows, cols, c_ends, batch)
        h = _post(l, h, rows, o_sb, o_fx, kvmem, *weights["post"], batch, qmem_col)
        h = _ffn(l, h, *weights["f2"])
    return h.reshape(batch, seq, d)
```

```python
import math

import jax
import jax.numpy as jnp
from jax import lax
from jax.experimental import pallas as pl
from jax.experimental.pallas import tpu as pltpu

F32 = jnp.float32
BF16 = jnp.bfloat16

HEAD_DIM = 64
N_SB_HEADS = 8
N_FOX_HEADS = 8
N_MEM_HEADS = 4
MEM_HEAD_DIM = 128
N_BRANCH = 3
RMS_EPS = 1e-6
LOG2_E = math.log2(math.e)

LANES = 128
VMEM_LIMIT_BYTES = 56 * 1024 * 1024
ROW_TILE = 512
ATT_TILE = 256
FOX_SUB_TILES = 4
SB_SUB_TILES = 2
KNORM_LANE = 32
EXP_DEAD = 93.0
EXP2_DEAD = 135.0
SOFTPLUS2_CLAMP = 86.0
NORM_SLACK = 1.01
MASKED = -1e30

_NT = (((1,), (1,)), ((), ()))


def _rms(x, g):
    return x * lax.rsqrt(jnp.mean(x * x, axis=-1, keepdims=True) + RMS_EPS) * g


def _params(*sem):
    return pltpu.CompilerParams(dimension_semantics=sem, vmem_limit_bytes=VMEM_LIMIT_BYTES)


def _resident(shape, index_map):
    return pl.BlockSpec(shape, index_map, pipeline_mode=pl.Buffered(1))


def _ffn_kernel(l_ref, h_ref, gpre_ref, gpost_ref, wg_ref, wu_ref, wd_ref, o_ref):
    x = h_ref[...]
    u = _rms(x, gpre_ref[...]).astype(BF16)
    g = jnp.dot(u, wg_ref[...], preferred_element_type=F32)
    up = jnp.dot(u, wu_ref[...], preferred_element_type=F32)
    a = (g * jax.nn.sigmoid(g) * up).astype(BF16)
    f = jnp.dot(a, wd_ref[...], preferred_element_type=F32)
    o_ref[...] = x + 0.5 * _rms(f, gpost_ref[...])


def _ffn(l, h, g_pre, g_post, w_gate, w_up, w_down):
    m, d = h.shape
    f = w_gate.shape[-1]
    tm = ROW_TILE
    wmap = lambda i, l: (l[0], 0, 0)
    return pl.pallas_call(
        _ffn_kernel,
        grid_spec=pltpu.PrefetchScalarGridSpec(
            num_scalar_prefetch=1,
            grid=(m // tm,),
            in_specs=[
                pl.BlockSpec((tm, d), lambda i, l: (i, 0)),
                _resident((None, 1, d), wmap),
                _resident((None, 1, d), wmap),
                _resident((None, d, f), wmap),
                _resident((None, d, f), wmap),
                _resident((None, f, d), wmap),
            ],
            out_specs=pl.BlockSpec((tm, d), lambda i, l: (i, 0)),
        ),
        out_shape=jax.ShapeDtypeStruct((m, d), F32),
        compiler_params=_params("parallel"),
        name="ffn",
    )(l, h, g_pre, g_post, w_gate, w_up, w_down)


def _split3(x):
    hi = x.astype(BF16).astype(F32)
    mid = (x - hi).astype(BF16).astype(F32)
    lo = ((x - hi) - mid).astype(BF16).astype(F32)
    return hi, mid, lo


def _fold3(x, nh):
    n = x.shape[1]
    return x + pltpu.roll(x, n - nh, axis=1) + pltpu.roll(x, n - 2 * nh, axis=1)


def _spread3(terms, nh):
    lane = lax.broadcasted_iota(jnp.int32, terms[0].shape, 1)
    out = jnp.where(lane < nh, terms[0], 0.0)
    for i in (1, 2):
        moved = pltpu.roll(terms[i], i * nh, axis=1)
        out = jnp.where(jnp.logical_and(lane >= i * nh, lane < (i + 1) * nh), moved, out)
    return out.astype(BF16)


def _proj_kernel(l_ref, h_ref, g_ref, w_ref, wt_ref, wg_ref, bf_ref, lower_ref, place_ref,
                 rows_ref, cols_ref, c_ref, carry_ref):
    @pl.when(pl.program_id(1) == 0)
    def _():
        carry_ref[...] = jnp.zeros_like(carry_ref)

    nh = N_FOX_HEADS
    u = _rms(h_ref[...], g_ref[...])
    u_hi, u_mid, u_lo = _split3(u)
    ub = u_hi.astype(BF16)

    w_gate3 = wg_ref[...]
    e_hi = jnp.dot(ub, w_gate3, preferred_element_type=F32)
    e_mid = jnp.dot(u_mid.astype(BF16), w_gate3, preferred_element_type=F32)
    e_lo = jnp.dot(u_lo.astype(BF16), w_gate3, preferred_element_type=F32)
    drop_lo = lax.broadcasted_iota(jnp.int32, e_mid.shape, 1) < 2 * nh
    fl = _fold3(e_hi + jnp.where(drop_lo, e_mid, 0.0), nh) + e_lo + bf_ref[...]
    log_f = jnp.minimum(fl, 0.0) - jnp.log1p(jnp.exp(-jnp.abs(fl)))

    sums = jnp.dot(lower_ref[...], _spread3(_split3(log_f), nh), preferred_element_type=F32)
    c = _fold3(sums, nh) + carry_ref[0:1, :]
    tm = c.shape[0]
    carry_ref[...] = jnp.broadcast_to(c[tm - 1:tm, :], carry_ref.shape)
    placed = jnp.dot(_spread3(_split3(c), nh), place_ref[...], preferred_element_type=F32)

    main = jnp.dot(ub, w_ref[...], preferred_element_type=F32)
    n_k = N_FOX_HEADS * HEAD_DIM
    kf = main[:, :n_k]
    dim = lax.broadcasted_iota(jnp.int32, (n_k, LANES), 0)
    lane_k = lax.broadcasted_iota(jnp.int32, (n_k, LANES), 1)
    head_of_dim = jnp.where(lane_k == KNORM_LANE + dim // HEAD_DIM, 1.0, 0.0).astype(BF16)
    norms2 = jnp.dot((kf * kf).astype(BF16), head_of_dim, preferred_element_type=F32)
    lane_c = lax.broadcasted_iota(jnp.int32, c.shape, 1)
    c_ref[...] = jnp.where(jnp.logical_and(lane_c >= KNORM_LANE, lane_c < KNORM_LANE + nh),
                           norms2, c)
    pair_w = 3 * LANES
    for p in range(n_k // LANES):
        base = p * pair_w
        rows_ref[:, base:base + LANES] = main[:, p * LANES:(p + 1) * LANES].astype(BF16)
        rows_ref[:, base + LANES:base + pair_w] = (
            placed[:, 2 * p * LANES:2 * (p + 1) * LANES].astype(BF16))
    rows_ref[:, (n_k // LANES) * pair_w:] = main[:, n_k:].astype(BF16)

    across = lax.dot_general(wt_ref[...], ub, _NT, preferred_element_type=F32).astype(BF16)
    t = ATT_TILE
    for s in range(cols_ref.shape[0]):
        cols_ref[s] = across[:, s * t:(s + 1) * t]


def _proj(l, h, g, w_main, w_t, w_gate3, b_f, lower, place, batch):
    m, d = h.shape
    seq = m // batch
    tm = ROW_TILE
    nt = seq // tm
    per = tm // ATT_TILE
    n_k = N_FOX_HEADS * HEAD_DIM
    n_rows_out = w_main.shape[-1] - n_k + (n_k // LANES) * 3 * LANES
    n_cols_out = w_t.shape[1]
    wmap = lambda b, i, l: (l[0], 0, 0)
    const = lambda b, i, l: (0, 0)
    return pl.pallas_call(
        _proj_kernel,
        grid_spec=pltpu.PrefetchScalarGridSpec(
            num_scalar_prefetch=1,
            grid=(batch, nt),
            in_specs=[
                pl.BlockSpec((tm, d), lambda b, i, l: (b * nt + i, 0)),
                _resident((None, 1, d), wmap),
                _resident((None, d, w_main.shape[-1]), wmap),
                _resident((None, n_cols_out, d), wmap),
                _resident((None, d, LANES), wmap),
                _resident((None, 1, LANES), wmap),
                _resident(lower.shape, const),
                _resident(place.shape, const),
            ],
            out_specs=[
                pl.BlockSpec((tm, n_rows_out), lambda b, i, l: (b * nt + i, 0)),
                pl.BlockSpec((None, per, n_cols_out, ATT_TILE), lambda b, i, l: (b, i, 0, 0)),
                pl.BlockSpec((tm, LANES), lambda b, i, l: (b * nt + i, 0)),
            ],
            scratch_shapes=[pltpu.VMEM((8, LANES), F32)],
        ),
        out_shape=[
            jax.ShapeDtypeStruct((m, n_rows_out), BF16),
            jax.ShapeDtypeStruct((batch, seq // ATT_TILE, n_cols_out, ATT_TILE), BF16),
            jax.ShapeDtypeStruct((m, LANES), F32),
        ],
        compiler_params=_params("parallel", "arbitrary"),
        name="proj",
    )(l, h, g, w_main, w_t, w_gate3, b_f, lower, place)


def _softplus2(z2):
    return jnp.maximum(z2, jnp.log2(1.0 + jnp.exp2(jnp.minimum(z2, SOFTPLUS2_CLAMP))))


def _sb_kernel(q_ref, k_ref, v_ref, tri_ref, o_ref, acc_ref):
    t = ATT_TILE
    qi = pl.program_id(2)
    lane = lax.broadcasted_iota(jnp.int32, (t, LANES), 1)
    own = (lane < HEAD_DIM, lane >= HEAD_DIM)
    n_sub = q_ref.shape[0] // t
    chains = [(s, hh) for s in range(n_sub) for hh in range(2)]
    n_ch = len(chains)
    diag = [n_sub * qi + s for s in range(n_sub)]
    q_heads = []
    for s, hh in chains:
        q = q_ref[s * t:(s + 1) * t, :]
        q_heads.append(jnp.where(own[hh], q, jnp.zeros_like(q)))
    tri = tri_ref[...]

    row = lax.broadcasted_iota(jnp.int32, (t, t), 0)
    col = lax.broadcasted_iota(jnp.int32, (t, t), 1)
    strictly_causal = col < row

    def scores(c, j):
        kb = k_ref[pl.ds(pl.multiple_of(j * t, t), t), :]
        return lax.dot_general(q_heads[c], kb, _NT, preferred_element_type=F32)

    def decay(z, mask):
        sp = _softplus2(z)
        if mask is not None:
            sp = jnp.where(mask, sp, 0.0)
        hi = sp.astype(BF16)
        lo = (sp - hi.astype(F32)).astype(BF16)
        later = jnp.dot(jnp.concatenate([hi, lo], axis=1), tri, preferred_element_type=F32)
        return sp, later, jnp.sum(sp, axis=1, keepdims=True)

    def weigh(z, sp, later, carry, mask, j):
        w = jnp.exp2((z - sp) - later - carry)
        if mask is not None:
            w = jnp.where(mask, w, 0.0)
        vb = v_ref[pl.ds(pl.multiple_of(j * t, t), t), :]
        return jnp.dot(w.astype(BF16), vb, preferred_element_type=F32)

    prev = [jnp.maximum(diag[s] - 1, 0) for s, hh in chains]
    zd = [scores(c, diag[s]) for c, (s, hh) in enumerate(chains)]
    zp = [scores(c, prev[c]) for c in range(n_ch)]
    dd = [decay(z, strictly_causal) for z in zd]
    dp = [decay(z, None) for z in zp]
    pvd = [weigh(zd[c], dd[c][0], dd[c][1], 0.0, strictly_causal, diag[s])
           for c, (s, hh) in enumerate(chains)]
    pvp = [weigh(zp[c], dp[c][0], dp[c][1], dd[c][2], None, prev[c]) for c in range(n_ch)]
    carries = []
    for c, (s, hh) in enumerate(chains):
        acc_ref[c] = pvd[c] + jnp.where(diag[s] > 0, pvp[c], 0.0)
        carries.append(dd[c][2] + dp[c][2])

    def alive(r_next, cs):
        more = [jnp.logical_and(diag[s] - r_next >= 0, jnp.min(cs[c]) < EXP2_DEAD)
                for c, (s, hh) in enumerate(chains)]
        any_more = more[0]
        for m in more[1:]:
            any_more = jnp.logical_or(any_more, m)
        return any_more.astype(jnp.int32)

    def cond(state):
        return state[1] > 0

    def body(state):
        r, cs = state[0], state[2:]
        js = [diag[s] - r for s, hh in chains]
        zs = [scores(c, jnp.maximum(js[c], 0)) for c in range(n_ch)]
        ds = [decay(z, None) for z in zs]
        new = []
        for c in range(n_ch):
            pv = weigh(zs[c], ds[c][0], ds[c][1], cs[c], None, jnp.maximum(js[c], 0))
            acc_ref[c] += jnp.where(js[c] >= 0, pv, 0.0)
            new.append(cs[c] + ds[c][2])
        return (r + 1, alive(r + 1, new)) + tuple(new)

    lax.while_loop(cond, body, (jnp.int32(2), alive(2, carries)) + tuple(carries))
    for s in range(n_sub):
        o_ref[s * t:(s + 1) * t, :] = jnp.where(
            own[0], acc_ref[2 * s], acc_ref[2 * s + 1]).astype(o_ref.dtype)


def _sb_attention(qkv, tri, batch, q_col, k_col, v_col):
    m = qkv.shape[0]
    seq = m // batch
    t = ATT_TILE
    n_sub = SB_SUB_TILES
    steps = seq // (n_sub * t)
    n_pairs = N_SB_HEADS * HEAD_DIM // LANES
    return pl.pallas_call(
        _sb_kernel,
        grid=(batch, n_pairs, steps),
        in_specs=[
            pl.BlockSpec((n_sub * t, LANES), lambda b, p, i: (b * steps + i, q_col + p)),
            pl.BlockSpec((seq, LANES), lambda b, p, i: (b, k_col + p)),
            pl.BlockSpec((seq, LANES), lambda b, p, i: (b, v_col + p)),
            _resident(tri.shape, lambda b, p, i: (0, 0)),
        ],
        out_specs=pl.BlockSpec((n_sub * t, LANES), lambda b, p, i: (b * steps + i, p)),
        out_shape=jax.ShapeDtypeStruct((m, n_pairs * LANES), BF16),
        scratch_shapes=[pltpu.VMEM((2 * n_sub, t, LANES), F32)],
        compiler_params=_params("parallel", "parallel", "arbitrary"),
        name="sb_attn",
    )(qkv, qkv, qkv, tri)


def _fox_kernel(qt_ref, k_ref, vt_ref, cend_ref, o_ref, y_ref, p_ref, acc_ref):
    t = ATT_TILE
    qi = pl.program_id(2)
    sub = lax.broadcasted_iota(jnp.int32, (LANES, t), 0)
    own = (sub < HEAD_DIM, sub >= HEAD_DIM)
    n_tiles = cend_ref.shape[0] - 1
    k_max = cend_ref[n_tiles:n_tiles + 1, :]

    n_sub = qt_ref.shape[0]
    chains = [(s, hh) for s in range(n_sub) for hh in range(2)]
    n_ch = len(chains)
    diag = [n_sub * qi + s for s in range(n_sub)]
    minus_c = jnp.where(sub < 3, -1.0, 0.0).astype(BF16)
    q_aug, z_bound = [], []
    for s, hh in chains:
        q_pair = qt_ref[s]
        qf = q_pair.astype(F32)
        z_bound.append(jnp.sqrt(jnp.sum(jnp.where(own[hh], qf * qf, 0.0), axis=0, keepdims=True))
                       * (k_max[:, hh:hh + 1] * NORM_SLACK))
        q_aug.append(jnp.concatenate([jnp.where(own[hh], q_pair, jnp.zeros_like(q_pair)), minus_c],
                                     axis=0))
    one = jnp.ones((LANES, t), BF16)

    def logits(c, j):
        kb = k_ref[pl.ds(pl.multiple_of(j * t, t), t), :]
        lhs = kb[:, :2 * LANES] if chains[c][1] == 0 else jnp.concatenate(
            [kb[:, :LANES], kb[:, 2 * LANES:]], axis=1)
        return jnp.dot(lhs, q_aug[c], preferred_element_type=F32)

    def softmax_step(y, m_prev):
        m_new = jnp.maximum(m_prev, jnp.max(y, axis=0, keepdims=True))
        return jnp.exp(y - m_new).astype(BF16), jnp.exp(m_prev - m_new), m_new

    def weighted_values(c, j):
        vt = jnp.where(own[chains[c][1]], vt_ref[j], one)
        return jnp.dot(vt, p_ref[c], preferred_element_type=F32)

    def alive(r_next, ms):
        done = []
        for c, (s, hh) in enumerate(chains):
            j = diag[s] - r_next
            far = cend_ref[pl.ds(jnp.maximum(j, 0), 1), :][:, hh:hh + 1]
            dead = jnp.max(z_bound[c] - far - ms[c]) < -EXP_DEAD
            done.append(jnp.logical_or(j < 0, dead))
        all_done = done[0]
        for d in done[1:]:
            all_done = jnp.logical_and(all_done, d)
        return jnp.logical_not(all_done).astype(jnp.int32)

    key = lax.broadcasted_iota(jnp.int32, (t, t), 0)
    query = lax.broadcasted_iota(jnp.int32, (t, t), 1)
    causal = key <= query
    yd = [logits(c, diag[s]) for c, (s, hh) in enumerate(chains)]
    yn = [logits(c, jnp.maximum(diag[s] - 1, 0)) for c, (s, hh) in enumerate(chains)]
    ms, alphas = [], []
    for c in range(n_ch):
        p, _, m_new = softmax_step(jnp.where(causal, yd[c], MASKED), jnp.full((1, t), MASKED, F32))
        p_ref[c] = p
        y_ref[c] = yn[c]
        acc_ref[c] = jnp.zeros((LANES, t), F32)
        ms.append(m_new)
        alphas.append(jnp.ones((1, t), F32))

    def cond(state):
        r, live = state[0], state[1]
        return jnp.logical_and(r <= diag[n_sub - 1], live > 0)

    def body(state):
        r = state[0]
        ms, alphas = state[2:2 + n_ch], state[2 + n_ch:]
        live = alive(r + 1, ms)
        pv = [weighted_values(c, jnp.maximum(diag[s] - r + 1, 0)) for c, (s, hh) in enumerate(chains)]
        yn = [logits(c, jnp.maximum(diag[s] - r - 1, 0)) for c, (s, hh) in enumerate(chains)]
        new_m, new_a = [], []
        for c, (s, hh) in enumerate(chains):
            y = y_ref[c]
            if s < n_sub - 1:
                y = y + jnp.where(diag[s] - r < 0, MASKED, 0.0)
            p, alpha, m_new = softmax_step(y, ms[c])
            acc_ref[c] = alphas[c] * acc_ref[c] + pv[c]
            p_ref[c] = p
            y_ref[c] = yn[c]
            new_m.append(m_new)
            new_a.append(alpha)
        return (r + 1, live) + tuple(new_m) + tuple(new_a)

    state = lax.while_loop(cond, body, (jnp.int32(1), alive(1, ms)) + tuple(ms) + tuple(alphas))
    r_end = state[0]
    for s in range(n_sub):
        outs = []
        for hh in range(2):
            c = 2 * s + hh
            acc = state[2 + n_ch + c] * acc_ref[c] + weighted_values(
                c, jnp.maximum(diag[s] - r_end + 1, 0))
            outs.append(acc / pltpu.roll(acc, HEAD_DIM, axis=0))
        o_ref[s * t:(s + 1) * t, :] = jnp.where(own[0], outs[0], outs[1]).T.astype(o_ref.dtype)


def _fox_attention(rows, cols, c_ends, batch):
    m = rows.shape[0]
    seq = m // batch
    t = ATT_TILE
    nq = seq // t
    n_pairs = N_FOX_HEADS * HEAD_DIM // LANES
    n_sub = FOX_SUB_TILES
    steps = nq // n_sub
    n_ch = 2 * n_sub
    return pl.pallas_call(
        _fox_kernel,
        grid=(batch, n_pairs, steps),
        in_specs=[
            pl.BlockSpec((None, n_sub, LANES, t), lambda b, p, i: (b, i, p, 0)),
            pl.BlockSpec((seq, 3 * LANES), lambda b, p, i: (b, p)),
            pl.BlockSpec((None, nq, LANES, t), lambda b, p, i: (b, 0, n_pairs + p, 0)),
            pl.BlockSpec((None, None, nq + 1, 2), lambda b, p, i: (b, p, 0, 0)),
        ],
        out_specs=pl.BlockSpec((n_sub * t, LANES), lambda b, p, i: (b * steps + i, p)),
        out_shape=jax.ShapeDtypeStruct((m, n_pairs * LANES), BF16),
        scratch_shapes=[
            pltpu.VMEM((n_ch, t, t), F32),
            pltpu.VMEM((n_ch, t, t), BF16),
            pltpu.VMEM((n_ch, LANES, t), F32),
        ],
        compiler_params=_params("parallel", "parallel", "arbitrary"),
        name="fox_attn",
    )(cols, rows, cols, c_ends)


def _memkv_kernel(mem_ref, g_ref, w_ref, o_ref):
    mem_n = _rms(mem_ref[...], g_ref[...]).astype(BF16)
    o_ref[...] = jnp.dot(mem_n, w_ref[...], preferred_element_type=F32).astype(BF16)


def _memkv(mem2d, g, w_mem_kv):
    n_layers, d, n_out = w_mem_kv.shape
    rows = mem2d.shape[0]
    return pl.pallas_call(
        _memkv_kernel,
        grid=(n_layers,),
        in_specs=[
            _resident((rows, d), lambda l: (0, 0)),
            _resident((1, d), lambda l: (0, 0)),
            pl.BlockSpec((None, d, n_out), lambda l: (l, 0, 0)),
        ],
        out_specs=pl.BlockSpec((None, rows, n_out), lambda l: (l, 0, 0)),
        out_shape=jax.ShapeDtypeStruct((n_layers, rows, n_out), BF16),
        compiler_params=_params("parallel"),
        name="memkv",
    )(mem2d, g, w_mem_kv)


def _post_kernel(l_ref, h_ref, qmem_ref, osb_ref, ofx_ref, kv_ref, gpre_ref, gpost_ref,
                 wgate_ref, bgate_ref, wsb_ref, wfx_ref, wmem_ref, wout_ref, o_ref):
    x = h_ref[...]
    d = x.shape[1]
    u = _rms(x, gpre_ref[...]).astype(BF16)
    gates = jax.nn.sigmoid(jnp.dot(u, wgate_ref[...], preferred_element_type=F32) + bgate_ref[...])

    qm = qmem_ref[...]
    kv = kv_ref[...]
    mem_w = N_MEM_HEADS * MEM_HEAD_DIM
    scale = MEM_HEAD_DIM ** -0.5
    heads = []
    for hh in range(N_MEM_HEADS):
        lo, hi = hh * MEM_HEAD_DIM, (hh + 1) * MEM_HEAD_DIM
        z = lax.dot_general(qm[:, lo:hi], kv[:, lo:hi], _NT, preferred_element_type=F32) * scale
        e = jnp.exp(z - jnp.max(z, axis=1, keepdims=True))
        p = (e / jnp.sum(e, axis=1, keepdims=True)).astype(BF16)
        heads.append(jnp.dot(p, kv[:, mem_w + lo:mem_w + hi], preferred_element_type=F32))
    o_mem = jnp.concatenate(heads, axis=1).astype(BF16)

    merged = (gates[:, 0:d] * jnp.dot(osb_ref[...], wsb_ref[...], preferred_element_type=F32)
              + gates[:, d:2 * d] * jnp.dot(ofx_ref[...], wfx_ref[...], preferred_element_type=F32)
              + gates[:, 2 * d:3 * d] * jnp.dot(o_mem, wmem_ref[...], preferred_element_type=F32))
    y = jnp.dot(merged.astype(BF16), wout_ref[...], preferred_element_type=F32)
    o_ref[...] = x + _rms(y, gpost_ref[...])


def _post(l, h, qkv, o_sb, o_fx, kvmem, g_pre, g_post, w_gate, b_gate, w_sb, w_fx, w_mem, w_out,
          batch, qmem_col):
    m, d = h.shape
    seq = m // batch
    tm = ROW_TILE
    nt = seq // tm
    mem_len = kvmem.shape[1] // batch
    mem_w = N_MEM_HEADS * MEM_HEAD_DIM
    wmap = lambda b, i, l: (l[0], 0, 0)
    rows = lambda b, i, l: (b * nt + i, 0)
    return pl.pallas_call(
        _post_kernel,
        grid_spec=pltpu.PrefetchScalarGridSpec(
            num_scalar_prefetch=1,
            grid=(batch, nt),
            in_specs=[
                pl.BlockSpec((tm, d), rows),
                pl.BlockSpec((tm, mem_w), lambda b, i, l: (b * nt + i, qmem_col)),
                pl.BlockSpec((tm, o_sb.shape[1]), rows),
                pl.BlockSpec((tm, o_fx.shape[1]), rows),
                pl.BlockSpec((None, mem_len, 2 * mem_w), lambda b, i, l: (l[0], b, 0)),
                _resident((None, 1, d), wmap),
                _resident((None, 1, d), wmap),
                _resident((None, d, N_BRANCH * d), wmap),
                _resident((None, 1, N_BRANCH * d), wmap),
                _resident((None,) + w_sb.shape[1:], wmap),
                _resident((None,) + w_fx.shape[1:], wmap),
                _resident((None,) + w_mem.shape[1:], wmap),
                _resident((None, d, d), wmap),
            ],
            out_specs=pl.BlockSpec((tm, d), rows),
        ),
        out_shape=jax.ShapeDtypeStruct((m, d), F32),
        compiler_params=_params("parallel", "parallel"),
        name="post",
    )(l, h, qkv, o_sb, o_fx, kvmem, g_pre, g_post, w_gate, b_gate, w_sb, w_fx, w_mem, w_out)


def kernel(x, mem, ffn1_pre_g, ffn1_post_g, ffn1_w_gate, ffn1_w_up, ffn1_w_down, mix_pre_g, mix_post_g, w_in, b_forget, mem_norm_g, w_mem_kv, w_gate, b_gate, w_br_sb, w_br_fox, w_br_mem, w_out, ffn2_pre_g, ffn2_post_g, ffn2_w_gate, ffn2_w_up, ffn2_w_down):
    batch, seq, d = x.shape
    n_layers = w_in.shape[0]
    sb_w = N_SB_HEADS * HEAD_DIM
    fox_w = N_FOX_HEADS * HEAD_DIM
    mem_w = N_MEM_HEADS * MEM_HEAD_DIM
    assert seq % ROW_TILE == 0 and seq % ATT_TILE == 0 and d % LANES == 0
    assert w_in.shape[2] == 3 * sb_w + 3 * fox_w + N_FOX_HEADS + mem_w

    scale = HEAD_DIM ** -0.5
    fx = 3 * sb_w
    f_lo = fx + 3 * fox_w
    f_hi = f_lo + N_FOX_HEADS
    bf = lambda w: w.astype(BF16)
    vec = lambda g: g[:, None, :]

    order = jnp.argsort(b_forget, axis=1)

    def by_head(w, axis):
        shape = [1] * w.ndim
        shape[0], shape[axis] = order.shape[0], order.shape[1]
        return jnp.take_along_axis(w, order.reshape(shape), axis=axis)

    def fox_cols(lo):
        w = w_in[:, :, lo:lo + fox_w].reshape(n_layers, d, N_FOX_HEADS, HEAD_DIM)
        return by_head(w, 2).reshape(n_layers, d, fox_w)

    w_gate_f = by_head(w_in[:, :, f_lo:f_hi], 2)
    g_hi = w_gate_f.astype(BF16)
    g_mid = (w_gate_f - g_hi.astype(F32)).astype(BF16)
    g_lo = ((w_gate_f - g_hi.astype(F32)) - g_mid.astype(F32)).astype(BF16)
    gate3 = jnp.pad(jnp.concatenate([g_hi, g_mid, g_lo], axis=2),
                    ((0, 0), (0, 0), (0, LANES - 3 * N_FOX_HEADS)))
    w_main = jnp.concatenate([bf(fox_cols(fx + fox_w)), bf(w_in[:, :, :sb_w] * (scale * LOG2_E)),
                              bf(w_in[:, :, sb_w:fx]), bf(w_in[:, :, f_hi:])], axis=2)
    w_t = jnp.swapaxes(jnp.concatenate([fox_cols(fx) * scale, fox_cols(fx + 2 * fox_w)], axis=2),
                       1, 2).astype(BF16)
    b_f = jnp.pad(by_head(b_forget, 1), ((0, 0), (0, LANES - N_FOX_HEADS)))[:, None, :]
    w_br_fox_sorted = by_head(w_br_fox.reshape(n_layers, N_FOX_HEADS, HEAD_DIM, d), 1).reshape(
        w_br_fox.shape)
    term, head = jnp.meshgrid(jnp.arange(3), jnp.arange(N_FOX_HEADS), indexing="ij")
    place = jnp.zeros((LANES, N_FOX_HEADS * LANES), BF16).at[
        term * N_FOX_HEADS + head, head * LANES + term].set(1.0)
    tok = jnp.arange(ROW_TILE)
    lower = (tok[:, None] >= tok[None, :]).astype(BF16)
    weights = dict(
        f1=(vec(ffn1_pre_g), vec(ffn1_post_g), bf(ffn1_w_gate), bf(ffn1_w_up), bf(ffn1_w_down)),
        f2=(vec(ffn2_pre_g), vec(ffn2_post_g), bf(ffn2_w_gate), bf(ffn2_w_up), bf(ffn2_w_down)),
        proj=(vec(mix_pre_g), w_main, w_t, gate3, b_f, lower, place),
        post=(vec(mix_pre_g), vec(mix_post_g), bf(w_gate), vec(b_gate), bf(w_br_sb),
              bf(w_br_fox_sorted), bf(w_br_mem), bf(w_out)),
    )

    t = ATT_TILE
    later = (jnp.arange(t)[:, None] > jnp.arange(t)[None, :]).astype(BF16)
    tri = jnp.concatenate([later, later], axis=0)

    kvmem = _memkv(mem.reshape(batch * mem.shape[1], d), mem_norm_g[None, :], bf(w_mem_kv))

    fox_k_w = (fox_w // LANES) * 3 * LANES
    q_sb_col = fox_k_w // LANES
    k_sb_col, v_sb_col = q_sb_col + sb_w // LANES, q_sb_col + 2 * sb_w // LANES
    qmem_col = (fox_k_w + fx) // mem_w
    nq = seq // t

    h = x.reshape(batch * seq, d)
    for layer in range(n_layers):
        l = jnp.full((1,), layer, jnp.int32)
        h = _ffn(l, h, *weights["f1"])
        rows, cols, c = _proj(l, h, *weights["proj"], batch)
        o_sb = _sb_attention(rows, tri, batch, q_sb_col, k_sb_col, v_sb_col)
        c = c.reshape(batch, nq, t, LANES)
        c_ends = c[:, :, t - 1, :N_FOX_HEADS]
        k_max = jnp.sqrt(jnp.max(c[:, :, :, KNORM_LANE:KNORM_LANE + N_FOX_HEADS], axis=(1, 2)))
        c_ends = jnp.concatenate([c_ends, k_max[:, None, :]], axis=1)
        c_ends = c_ends.reshape(batch, nq + 1, N_FOX_HEADS // 2, 2).transpose(0, 2, 1, 3)
        o_fx = _fox_attention(rows, cols, c_ends, batch)
        h = _post(l, h, rows, o_sb, o_fx, kvmem, *weights["post"], batch, qmem_col)
        h = _ffn(l, h, *weights["f2"])
    return h.reshape(batch, seq, d)
```

```python
import math

import jax
import jax.numpy as jnp
from jax import lax
from jax.experimental import pallas as pl
from jax.experimental.pallas import tpu as pltpu

F32 = jnp.float32
BF16 = jnp.bfloat16

HEAD_DIM = 64
N_SB_HEADS = 8
N_FOX_HEADS = 8
N_MEM_HEADS = 4
MEM_HEAD_DIM = 128
N_BRANCH = 3
RMS_EPS = 1e-6
LOG2_E = math.log2(math.e)

LANES = 128
MXU_WIDTH = 256
VMEM_LIMIT_BYTES = 56 * 1024 * 1024
ROW_TILE = 512
WIDE_ROW_TILE = 1024
FFN_CHUNKS = 2
ATT_TILE = 256
FOX_SUB_TILES = 4
SB_SUB_TILES = 2
KNORM_LANE = 32
EXP_DEAD = 93.0
EXP2_DEAD = 135.0
SOFTPLUS2_CLAMP = 86.0
NORM_SLACK = 1.01
MASKED = -1e30

_NT = (((1,), (1,)), ((), ()))


def _rms(x, g):
    return x * lax.rsqrt(jnp.mean(x * x, axis=-1, keepdims=True) + RMS_EPS) * g


def _params(*sem):
    return pltpu.CompilerParams(dimension_semantics=sem, vmem_limit_bytes=VMEM_LIMIT_BYTES)


def _resident(shape, index_map):
    return pl.BlockSpec(shape, index_map, pipeline_mode=pl.Buffered(1))


def _ffn_kernel(l_ref, h_ref, gpre_ref, gpost_ref, wg_ref, wu_ref, wd_ref, o_ref):
    x = h_ref[...]
    u = _rms(x, gpre_ref[...]).astype(BF16)
    n_mxu = wg_ref.shape[1] // MXU_WIDTH
    cuts = [MXU_WIDTH * ((n_mxu * k + FFN_CHUNKS - 1) // FFN_CHUNKS) for k in range(FFN_CHUNKS + 1)]
    f = None
    for k in range(FFN_CHUNKS):
        cols = slice(cuts[k], cuts[k + 1])
        g = jnp.dot(u, wg_ref[:, cols], preferred_element_type=F32)
        up = jnp.dot(u, wu_ref[:, cols], preferred_element_type=F32)
        a = (g * jax.nn.sigmoid(g) * up).astype(BF16)
        part = jnp.dot(a, wd_ref[cols, :], preferred_element_type=F32)
        f = part if f is None else f + part
    o_ref[...] = x + 0.5 * _rms(f, gpost_ref[...])


def _ffn(l, h, g_pre, g_post, w_gate, w_up, w_down):
    m, d = h.shape
    f = w_gate.shape[-1]
    tm = WIDE_ROW_TILE
    wmap = lambda i, l: (l[0], 0, 0)
    return pl.pallas_call(
        _ffn_kernel,
        grid_spec=pltpu.PrefetchScalarGridSpec(
            num_scalar_prefetch=1,
            grid=(m // tm,),
            in_specs=[
                pl.BlockSpec((tm, d), lambda i, l: (i, 0)),
                _resident((None, 1, d), wmap),
                _resident((None, 1, d), wmap),
                _resident((None, d, f), wmap),
                _resident((None, d, f), wmap),
                _resident((None, f, d), wmap),
            ],
            out_specs=pl.BlockSpec((tm, d), lambda i, l: (i, 0)),
        ),
        out_shape=jax.ShapeDtypeStruct((m, d), F32),
        compiler_params=_params("parallel"),
        name="ffn",
    )(l, h, g_pre, g_post, w_gate, w_up, w_down)


def _split3(x):
    hi = x.astype(BF16).astype(F32)
    mid = (x - hi).astype(BF16).astype(F32)
    lo = ((x - hi) - mid).astype(BF16).astype(F32)
    return hi, mid, lo


def _fold3(x, nh):
    n = x.shape[1]
    return x + pltpu.roll(x, n - nh, axis=1) + pltpu.roll(x, n - 2 * nh, axis=1)


def _spread3(terms, nh):
    lane = lax.broadcasted_iota(jnp.int32, terms[0].shape, 1)
    out = jnp.where(lane < nh, terms[0], 0.0)
    for i in (1, 2):
        moved = pltpu.roll(terms[i], i * nh, axis=1)
        out = jnp.where(jnp.logical_and(lane >= i * nh, lane < (i + 1) * nh), moved, out)
    return out.astype(BF16)


def _proj_kernel(l_ref, h_ref, g_ref, w_ref, wt_ref, wg_ref, bf_ref, lower_ref, place_ref,
                 rows_ref, cols_ref, c_ref, carry_ref):
    @pl.when(pl.program_id(1) == 0)
    def _():
        carry_ref[...] = jnp.zeros_like(carry_ref)

    nh = N_FOX_HEADS
    u = _rms(h_ref[...], g_ref[...])
    u_hi, u_mid, u_lo = _split3(u)
    ub = u_hi.astype(BF16)

    w_gate3 = wg_ref[...]
    e_hi = jnp.dot(ub, w_gate3, preferred_element_type=F32)
    e_mid = jnp.dot(u_mid.astype(BF16), w_gate3, preferred_element_type=F32)
    e_lo = jnp.dot(u_lo.astype(BF16), w_gate3, preferred_element_type=F32)
    drop_lo = lax.broadcasted_iota(jnp.int32, e_mid.shape, 1) < 2 * nh
    fl = _fold3(e_hi + jnp.where(drop_lo, e_mid, 0.0), nh) + e_lo + bf_ref[...]
    log_f = jnp.minimum(fl, 0.0) - jnp.log1p(jnp.exp(-jnp.abs(fl)))

    sums = jnp.dot(lower_ref[...], _spread3(_split3(log_f), nh), preferred_element_type=F32)
    c = _fold3(sums, nh) + carry_ref[0:1, :]
    tm = c.shape[0]
    carry_ref[...] = jnp.broadcast_to(c[tm - 1:tm, :], carry_ref.shape)
    placed = jnp.dot(_spread3(_split3(c), nh), place_ref[...], preferred_element_type=F32)

    main = jnp.dot(ub, w_ref[...], preferred_element_type=F32)
    n_k = N_FOX_HEADS * HEAD_DIM
    kf = main[:, :n_k]
    dim = lax.broadcasted_iota(jnp.int32, (n_k, LANES), 0)
    lane_k = lax.broadcasted_iota(jnp.int32, (n_k, LANES), 1)
    head_of_dim = jnp.where(lane_k == KNORM_LANE + dim // HEAD_DIM, 1.0, 0.0).astype(BF16)
    norms2 = jnp.dot((kf * kf).astype(BF16), head_of_dim, preferred_element_type=F32)
    lane_c = lax.broadcasted_iota(jnp.int32, c.shape, 1)
    c_ref[...] = jnp.where(jnp.logical_and(lane_c >= KNORM_LANE, lane_c < KNORM_LANE + nh),
                           norms2, c)
    pair_w = 3 * LANES
    for p in range(n_k // LANES):
        base = p * pair_w
        rows_ref[:, base:base + LANES] = main[:, p * LANES:(p + 1) * LANES].astype(BF16)
        rows_ref[:, base + LANES:base + pair_w] = (
            placed[:, 2 * p * LANES:2 * (p + 1) * LANES].astype(BF16))
    rows_ref[:, (n_k // LANES) * pair_w:] = main[:, n_k:].astype(BF16)

    across = lax.dot_general(wt_ref[...], ub, _NT, preferred_element_type=F32).astype(BF16)
    t = ATT_TILE
    for s in range(cols_ref.shape[0]):
        cols_ref[s] = across[:, s * t:(s + 1) * t]


def _proj(l, h, g, w_main, w_t, w_gate3, b_f, lower, place, batch):
    m, d = h.shape
    seq = m // batch
    tm = ROW_TILE
    nt = seq // tm
    per = tm // ATT_TILE
    n_k = N_FOX_HEADS * HEAD_DIM
    n_rows_out = w_main.shape[-1] - n_k + (n_k // LANES) * 3 * LANES
    n_cols_out = w_t.shape[1]
    wmap = lambda b, i, l: (l[0], 0, 0)
    const = lambda b, i, l: (0, 0)
    return pl.pallas_call(
        _proj_kernel,
        grid_spec=pltpu.PrefetchScalarGridSpec(
            num_scalar_prefetch=1,
            grid=(batch, nt),
            in_specs=[
                pl.BlockSpec((tm, d), lambda b, i, l: (b * nt + i, 0)),
                _resident((None, 1, d), wmap),
                _resident((None, d, w_main.shape[-1]), wmap),
                _resident((None, n_cols_out, d), wmap),
                _resident((None, d, LANES), wmap),
                _resident((None, 1, LANES), wmap),
                _resident(lower.shape, const),
                _resident(place.shape, const),
            ],
            out_specs=[
                pl.BlockSpec((tm, n_rows_out), lambda b, i, l: (b * nt + i, 0)),
                pl.BlockSpec((None, per, n_cols_out, ATT_TILE), lambda b, i, l: (b, i, 0, 0)),
                pl.BlockSpec((tm, LANES), lambda b, i, l: (b * nt + i, 0)),
            ],
            scratch_shapes=[pltpu.VMEM((8, LANES), F32)],
        ),
        out_shape=[
            jax.ShapeDtypeStruct((m, n_rows_out), BF16),
            jax.ShapeDtypeStruct((batch, seq // ATT_TILE, n_cols_out, ATT_TILE), BF16),
            jax.ShapeDtypeStruct((m, LANES), F32),
        ],
        compiler_params=_params("parallel", "arbitrary"),
        name="proj",
    )(l, h, g, w_main, w_t, w_gate3, b_f, lower, place)


def _softplus2(z2):
    return jnp.maximum(z2, jnp.log2(1.0 + jnp.exp2(jnp.minimum(z2, SOFTPLUS2_CLAMP))))


def _sb_kernel(qt_ref, k_ref, vt_ref, tri_ref, o_ref, acc_ref):
    t = ATT_TILE
    qi = pl.program_id(2)
    sub = lax.broadcasted_iota(jnp.int32, (LANES, t), 0)
    own = (sub < HEAD_DIM, sub >= HEAD_DIM)
    n_sub = qt_ref.shape[0]
    chains = [(s, hh) for s in range(n_sub) for hh in range(2)]
    n_ch = len(chains)
    diag = [n_sub * qi + s for s in range(n_sub)]
    q_heads = []
    for s, hh in chains:
        q_pair = qt_ref[s]
        q_heads.append(jnp.where(own[hh], q_pair, jnp.zeros_like(q_pair)))
    tri = tri_ref[...]

    key = lax.broadcasted_iota(jnp.int32, (t, t), 0)
    query = lax.broadcasted_iota(jnp.int32, (t, t), 1)
    strictly_causal = key < query

    def scores(c, j, mask):
        kb = k_ref[pl.ds(pl.multiple_of(j * t, t), t), :]
        z = jnp.dot(kb, q_heads[c], preferred_element_type=F32)
        return z if mask is None else jnp.where(mask, z, MASKED)

    def decay(z):
        sp = _softplus2(z)
        hi = sp.astype(BF16)
        lo = (sp - hi.astype(F32)).astype(BF16)
        later = jnp.dot(tri, jnp.concatenate([hi, lo], axis=0), preferred_element_type=F32)
        return sp, later, jnp.sum(sp, axis=0, keepdims=True)

    def weigh(z, sp, later, carry, j):
        w = jnp.exp2((z - sp) - later - carry)
        return jnp.dot(vt_ref[j], w.astype(BF16), preferred_element_type=F32)

    prev = [jnp.maximum(diag[s] - 1, 0) for s, hh in chains]
    zd = [scores(c, diag[s], strictly_causal) for c, (s, hh) in enumerate(chains)]
    zp = [scores(c, prev[c], None) for c in range(n_ch)]
    dd = [decay(z) for z in zd]
    dp = [decay(z) for z in zp]
    pvd = [weigh(zd[c], dd[c][0], dd[c][1], 0.0, diag[s]) for c, (s, hh) in enumerate(chains)]
    pvp = [weigh(zp[c], dp[c][0], dp[c][1], dd[c][2], prev[c]) for c in range(n_ch)]
    carries = []
    for c, (s, hh) in enumerate(chains):
        acc_ref[c] = pvd[c] + jnp.where(diag[s] > 0, pvp[c], 0.0)
        carries.append(dd[c][2] + dp[c][2])

    def alive(r_next, cs):
        more = [jnp.logical_and(diag[s] - r_next >= 0, jnp.min(cs[c]) < EXP2_DEAD)
                for c, (s, hh) in enumerate(chains)]
        any_more = more[0]
        for m in more[1:]:
            any_more = jnp.logical_or(any_more, m)
        return any_more.astype(jnp.int32)

    def cond(state):
        return state[1] > 0

    def body(state):
        r, cs = state[0], state[2:]
        js = [diag[s] - r for s, hh in chains]
        zs = [scores(c, jnp.maximum(js[c], 0), None) for c in range(n_ch)]
        ds = [decay(z) for z in zs]
        new = []
        for c in range(n_ch):
            pv = weigh(zs[c], ds[c][0], ds[c][1], cs[c], jnp.maximum(js[c], 0))
            acc_ref[c] += jnp.where(js[c] >= 0, pv, 0.0)
            new.append(cs[c] + ds[c][2])
        return (r + 1, alive(r + 1, new)) + tuple(new)

    lax.while_loop(cond, body, (jnp.int32(2), alive(2, carries)) + tuple(carries))
    for s in range(n_sub):
        o_ref[s * t:(s + 1) * t, :] = jnp.where(
            own[0], acc_ref[2 * s], acc_ref[2 * s + 1]).T.astype(o_ref.dtype)


def _sb_attention(rows, cols, tri, batch, k_col, q_row, v_row):
    m = rows.shape[0]
    seq = m // batch
    t = ATT_TILE
    nq = seq // t
    n_sub = SB_SUB_TILES
    steps = nq // n_sub
    n_pairs = N_SB_HEADS * HEAD_DIM // LANES
    return pl.pallas_call(
        _sb_kernel,
        grid=(batch, n_pairs, steps),
        in_specs=[
            pl.BlockSpec((None, n_sub, LANES, t), lambda b, p, i: (b, i, q_row + p, 0)),
            pl.BlockSpec((seq, LANES), lambda b, p, i: (b, k_col + p)),
            pl.BlockSpec((None, nq, LANES, t), lambda b, p, i: (b, 0, v_row + p, 0)),
            _resident(tri.shape, lambda b, p, i: (0, 0)),
        ],
        out_specs=pl.BlockSpec((n_sub * t, LANES), lambda b, p, i: (b * steps + i, p)),
        out_shape=jax.ShapeDtypeStruct((m, n_pairs * LANES), BF16),
        scratch_shapes=[pltpu.VMEM((2 * n_sub, LANES, t), F32)],
        compiler_params=_params("parallel", "parallel", "arbitrary"),
        name="sb_attn",
    )(cols, rows, cols, tri)


def _fox_kernel(qt_ref, k_ref, vt_ref, cend_ref, o_ref, y_ref, p_ref, acc_ref):
    t = ATT_TILE
    qi = pl.program_id(2)
    sub = lax.broadcasted_iota(jnp.int32, (LANES, t), 0)
    own = (sub < HEAD_DIM, sub >= HEAD_DIM)
    n_tiles = cend_ref.shape[0] - 1
    k_max = cend_ref[n_tiles:n_tiles + 1, :]

    n_sub = qt_ref.shape[0]
    chains = [(s, hh) for s in range(n_sub) for hh in range(2)]
    n_ch = len(chains)
    diag = [n_sub * qi + s for s in range(n_sub)]
    minus_c = jnp.where(sub < 3, -1.0, 0.0).astype(BF16)
    q_aug, z_bound = [], []
    for s, hh in chains:
        q_pair = qt_ref[s]
        qf = q_pair.astype(F32)
        z_bound.append(jnp.sqrt(jnp.sum(jnp.where(own[hh], qf * qf, 0.0), axis=0, keepdims=True))
                       * (k_max[:, hh:hh + 1] * NORM_SLACK))
        q_aug.append(jnp.concatenate([jnp.where(own[hh], q_pair, jnp.zeros_like(q_pair)), minus_c],
                                     axis=0))
    one = jnp.ones((LANES, t), BF16)

    def logits(c, j):
        kb = k_ref[pl.ds(pl.multiple_of(j * t, t), t), :]
        lhs = kb[:, :2 * LANES] if chains[c][1] == 0 else jnp.concatenate(
            [kb[:, :LANES], kb[:, 2 * LANES:]], axis=1)
        return jnp.dot(lhs, q_aug[c], preferred_element_type=F32)

    def softmax_step(y, m_prev):
        m_new = jnp.maximum(m_prev, jnp.max(y, axis=0, keepdims=True))
        return jnp.exp(y - m_new).astype(BF16), jnp.exp(m_prev - m_new), m_new

    def weighted_values(c, j):
        vt = jnp.where(own[chains[c][1]], vt_ref[j], one)
        return jnp.dot(vt, p_ref[c], preferred_element_type=F32)

    def alive(r_next, ms):
        done = []
        for c, (s, hh) in enumerate(chains):
            j = diag[s] - r_next
            far = cend_ref[pl.ds(jnp.maximum(j, 0), 1), :][:, hh:hh + 1]
            dead = jnp.max(z_bound[c] - far - ms[c]) < -EXP_DEAD
            done.append(jnp.logical_or(j < 0, dead))
        all_done = done[0]
        for d in done[1:]:
            all_done = jnp.logical_and(all_done, d)
        return jnp.logical_not(all_done).astype(jnp.int32)

    key = lax.broadcasted_iota(jnp.int32, (t, t), 0)
    query = lax.broadcasted_iota(jnp.int32, (t, t), 1)
    causal = key <= query
    yd = [logits(c, diag[s]) for c, (s, hh) in enumerate(chains)]
    yn = [logits(c, jnp.maximum(diag[s] - 1, 0)) for c, (s, hh) in enumerate(chains)]
    ms, alphas = [], []
    for c in range(n_ch):
        p, _, m_new = softmax_step(jnp.where(causal, yd[c], MASKED), jnp.full((1, t), MASKED, F32))
        p_ref[c] = p
        y_ref[c] = yn[c]
        acc_ref[c] = jnp.zeros((LANES, t), F32)
        ms.append(m_new)
        alphas.append(jnp.ones((1, t), F32))

    def cond(state):
        r, live = state[0], state[1]
        return jnp.logical_and(r <= diag[n_sub - 1], live > 0)

    def body(state):
        r = state[0]
        ms, alphas = state[2:2 + n_ch], state[2 + n_ch:]
        live = alive(r + 1, ms)
        pv = [weighted_values(c, jnp.maximum(diag[s] - r + 1, 0)) for c, (s, hh) in enumerate(chains)]
        yn = [logits(c, jnp.maximum(diag[s] - r - 1, 0)) for c, (s, hh) in enumerate(chains)]
        new_m, new_a = [], []
        for c, (s, hh) in enumerate(chains):
            y = y_ref[c]
            if s < n_sub - 1:
                y = y + jnp.where(diag[s] - r < 0, MASKED, 0.0)
            p, alpha, m_new = softmax_step(y, ms[c])
            acc_ref[c] = alphas[c] * acc_ref[c] + pv[c]
            p_ref[c] = p
            y_ref[c] = yn[c]
            new_m.append(m_new)
            new_a.append(alpha)
        return (r + 1, live) + tuple(new_m) + tuple(new_a)

    state = lax.while_loop(cond, body, (jnp.int32(1), alive(1, ms)) + tuple(ms) + tuple(alphas))
    r_end = state[0]
    for s in range(n_sub):
        outs = []
        for hh in range(2):
            c = 2 * s + hh
            acc = state[2 + n_ch + c] * acc_ref[c] + weighted_values(
                c, jnp.maximum(diag[s] - r_end + 1, 0))
            outs.append(acc / pltpu.roll(acc, HEAD_DIM, axis=0))
        o_ref[s * t:(s + 1) * t, :] = jnp.where(own[0], outs[0], outs[1]).T.astype(o_ref.dtype)


def _fox_attention(rows, cols, c_ends, batch):
    m = rows.shape[0]
    seq = m // batch
    t = ATT_TILE
    nq = seq // t
    n_pairs = N_FOX_HEADS * HEAD_DIM // LANES
    n_sub = FOX_SUB_TILES
    steps = nq // n_sub
    n_ch = 2 * n_sub
    return pl.pallas_call(
        _fox_kernel,
        grid=(batch, n_pairs, steps),
        in_specs=[
            pl.BlockSpec((None, n_sub, LANES, t), lambda b, p, i: (b, i, p, 0)),
            pl.BlockSpec((seq, 3 * LANES), lambda b, p, i: (b, p)),
            pl.BlockSpec((None, nq, LANES, t), lambda b, p, i: (b, 0, n_pairs + p, 0)),
            pl.BlockSpec((None, None, nq + 1, 2), lambda b, p, i: (b, p, 0, 0)),
        ],
        out_specs=pl.BlockSpec((n_sub * t, LANES), lambda b, p, i: (b * steps + i, p)),
        out_shape=jax.ShapeDtypeStruct((m, n_pairs * LANES), BF16),
        scratch_shapes=[
            pltpu.VMEM((n_ch, t, t), F32),
            pltpu.VMEM((n_ch, t, t), BF16),
            pltpu.VMEM((n_ch, LANES, t), F32),
        ],
        compiler_params=_params("parallel", "parallel", "arbitrary"),
        name="fox_attn",
    )(cols, rows, cols, c_ends)


def _memkv_kernel(mem_ref, g_ref, w_ref, o_ref):
    mem_n = _rms(mem_ref[...], g_ref[...]).astype(BF16)
    o_ref[...] = jnp.dot(mem_n, w_ref[...], preferred_element_type=F32).astype(BF16)


def _memkv(mem2d, g, w_mem_kv):
    n_layers, d, n_out = w_mem_kv.shape
    rows = mem2d.shape[0]
    return pl.pallas_call(
        _memkv_kernel,
        grid=(n_layers,),
        in_specs=[
            _resident((rows, d), lambda l: (0, 0)),
            _resident((1, d), lambda l: (0, 0)),
            pl.BlockSpec((None, d, n_out), lambda l: (l, 0, 0)),
        ],
        out_specs=pl.BlockSpec((None, rows, n_out), lambda l: (l, 0, 0)),
        out_shape=jax.ShapeDtypeStruct((n_layers, rows, n_out), BF16),
        compiler_params=_params("parallel"),
        name="memkv",
    )(mem2d, g, w_mem_kv)


def _post_kernel(l_ref, h_ref, qmem_ref, osb_ref, ofx_ref, kv_ref, gpre_ref, gpost_ref,
                 wgate_ref, bgate_ref, wsb_ref, wfx_ref, wmem_ref, wout_ref, o_ref):
    x = h_ref[...]
    d = x.shape[1]
    u = _rms(x, gpre_ref[...]).astype(BF16)
    gates = jax.nn.sigmoid(jnp.dot(u, wgate_ref[...], preferred_element_type=F32) + bgate_ref[...])

    qm = qmem_ref[...]
    kv = kv_ref[...]
    mem_w = N_MEM_HEADS * MEM_HEAD_DIM
    scale = MEM_HEAD_DIM ** -0.5
    heads = []
    for hh in range(N_MEM_HEADS):
        lo, hi = hh * MEM_HEAD_DIM, (hh + 1) * MEM_HEAD_DIM
        z = lax.dot_general(qm[:, lo:hi], kv[:, lo:hi], _NT, preferred_element_type=F32) * scale
        e = jnp.exp(z - jnp.max(z, axis=1, keepdims=True))
        p = (e / jnp.sum(e, axis=1, keepdims=True)).astype(BF16)
        heads.append(jnp.dot(p, kv[:, mem_w + lo:mem_w + hi], preferred_element_type=F32))
    o_mem = jnp.concatenate(heads, axis=1).astype(BF16)

    merged = (gates[:, 0:d] * jnp.dot(osb_ref[...], wsb_ref[...], preferred_element_type=F32)
              + gates[:, d:2 * d] * jnp.dot(ofx_ref[...], wfx_ref[...], preferred_element_type=F32)
              + gates[:, 2 * d:3 * d] * jnp.dot(o_mem, wmem_ref[...], preferred_element_type=F32))
    y = jnp.dot(merged.astype(BF16), wout_ref[...], preferred_element_type=F32)
    o_ref[...] = x + _rms(y, gpost_ref[...])


def _post(l, h, qkv, o_sb, o_fx, kvmem, g_pre, g_post, w_gate, b_gate, w_sb, w_fx, w_mem, w_out,
          batch, qmem_col):
    m, d = h.shape
    seq = m // batch
    tm = WIDE_ROW_TILE
    nt = seq // tm
    mem_len = kvmem.shape[1] // batch
    mem_w = N_MEM_HEADS * MEM_HEAD_DIM
    wmap = lambda b, i, l: (l[0], 0, 0)
    rows = lambda b, i, l: (b * nt + i, 0)
    return pl.pallas_call(
        _post_kernel,
        grid_spec=pltpu.PrefetchScalarGridSpec(
            num_scalar_prefetch=1,
            grid=(batch, nt),
            in_specs=[
                pl.BlockSpec((tm, d), rows),
                pl.BlockSpec((tm, mem_w), lambda b, i, l: (b * nt + i, qmem_col)),
                pl.BlockSpec((tm, o_sb.shape[1]), rows),
                pl.BlockSpec((tm, o_fx.shape[1]), rows),
                pl.BlockSpec((None, mem_len, 2 * mem_w), lambda b, i, l: (l[0], b, 0)),
                _resident((None, 1, d), wmap),
                _resident((None, 1, d), wmap),
                _resident((None, d, N_BRANCH * d), wmap),
                _resident((None, 1, N_BRANCH * d), wmap),
                _resident((None,) + w_sb.shape[1:], wmap),
                _resident((None,) + w_fx.shape[1:], wmap),
                _resident((None,) + w_mem.shape[1:], wmap),
                _resident((None, d, d), wmap),
            ],
            out_specs=pl.BlockSpec((tm, d), rows),
        ),
        out_shape=jax.ShapeDtypeStruct((m, d), F32),
        compiler_params=_params("parallel", "parallel"),
        name="post",
    )(l, h, qkv, o_sb, o_fx, kvmem, g_pre, g_post, w_gate, b_gate, w_sb, w_fx, w_mem, w_out)


def kernel(x, mem, ffn1_pre_g, ffn1_post_g, ffn1_w_gate, ffn1_w_up, ffn1_w_down, mix_pre_g, mix_post_g, w_in, b_forget, mem_norm_g, w_mem_kv, w_gate, b_gate, w_br_sb, w_br_fox, w_br_mem, w_out, ffn2_pre_g, ffn2_post_g, ffn2_w_gate, ffn2_w_up, ffn2_w_down):
    batch, seq, d = x.shape
    n_layers = w_in.shape[0]
    sb_w = N_SB_HEADS * HEAD_DIM
    fox_w = N_FOX_HEADS * HEAD_DIM
    mem_w = N_MEM_HEADS * MEM_HEAD_DIM
    assert seq % WIDE_ROW_TILE == 0 and WIDE_ROW_TILE % ROW_TILE == 0 and d % LANES == 0
    assert seq % (max(FOX_SUB_TILES, SB_SUB_TILES) * ATT_TILE) == 0 and ROW_TILE % ATT_TILE == 0
    assert w_in.shape[2] == 3 * sb_w + 3 * fox_w + N_FOX_HEADS + mem_w

    scale = HEAD_DIM ** -0.5
    fx = 3 * sb_w
    f_lo = fx + 3 * fox_w
    f_hi = f_lo + N_FOX_HEADS
    bf = lambda w: w.astype(BF16)
    vec = lambda g: g[:, None, :]

    order = jnp.argsort(b_forget, axis=1)

    def by_head(w, axis):
        shape = [1] * w.ndim
        shape[0], shape[axis] = order.shape[0], order.shape[1]
        return jnp.take_along_axis(w, order.reshape(shape), axis=axis)

    def fox_cols(lo):
        w = w_in[:, :, lo:lo + fox_w].reshape(n_layers, d, N_FOX_HEADS, HEAD_DIM)
        return by_head(w, 2).reshape(n_layers, d, fox_w)

    w_gate_f = by_head(w_in[:, :, f_lo:f_hi], 2)
    g_hi = w_gate_f.astype(BF16)
    g_mid = (w_gate_f - g_hi.astype(F32)).astype(BF16)
    g_lo = ((w_gate_f - g_hi.astype(F32)) - g_mid.astype(F32)).astype(BF16)
    gate3 = jnp.pad(jnp.concatenate([g_hi, g_mid, g_lo], axis=2),
                    ((0, 0), (0, 0), (0, LANES - 3 * N_FOX_HEADS)))
    w_main = jnp.concatenate([bf(fox_cols(fx + fox_w)), bf(w_in[:, :, sb_w:2 * sb_w]),
                              bf(w_in[:, :, f_hi:])], axis=2)
    w_t = jnp.swapaxes(jnp.concatenate(
        [fox_cols(fx) * scale, fox_cols(fx + 2 * fox_w),
         w_in[:, :, :sb_w] * (scale * LOG2_E), w_in[:, :, 2 * sb_w:fx]], axis=2), 1, 2).astype(BF16)
    b_f = jnp.pad(by_head(b_forget, 1), ((0, 0), (0, LANES - N_FOX_HEADS)))[:, None, :]
    w_br_fox_sorted = by_head(w_br_fox.reshape(n_layers, N_FOX_HEADS, HEAD_DIM, d), 1).reshape(
        w_br_fox.shape)
    term, head = jnp.meshgrid(jnp.arange(3), jnp.arange(N_FOX_HEADS), indexing="ij")
    place = jnp.zeros((LANES, N_FOX_HEADS * LANES), BF16).at[
        term * N_FOX_HEADS + head, head * LANES + term].set(1.0)
    tok = jnp.arange(ROW_TILE)
    lower = (tok[:, None] >= tok[None, :]).astype(BF16)
    weights = dict(
        f1=(vec(ffn1_pre_g), vec(ffn1_post_g), bf(ffn1_w_gate), bf(ffn1_w_up), bf(ffn1_w_down)),
        f2=(vec(ffn2_pre_g), vec(ffn2_post_g), bf(ffn2_w_gate), bf(ffn2_w_up), bf(ffn2_w_down)),
        proj=(vec(mix_pre_g), w_main, w_t, gate3, b_f, lower, place),
        post=(vec(mix_pre_g), vec(mix_post_g), bf(w_gate), vec(b_gate), bf(w_br_sb),
              bf(w_br_fox_sorted), bf(w_br_mem), bf(w_out)),
    )

    t = ATT_TILE
    upper = (jnp.arange(t)[None, :] > jnp.arange(t)[:, None]).astype(BF16)
    tri = jnp.concatenate([upper, upper], axis=1)

    kvmem = _memkv(mem.reshape(batch * mem.shape[1], d), mem_norm_g[None, :], bf(w_mem_kv))

    fox_k_w = (fox_w // LANES) * 3 * LANES
    k_sb_col = fox_k_w // LANES
    qmem_col = (fox_k_w + sb_w) // mem_w
    q_sb_row = 2 * fox_w // LANES
    v_sb_row = q_sb_row + sb_w // LANES
    nq = seq // t

    h = x.reshape(batch * seq, d)
    for layer in range(n_layers):
        l = jnp.full((1,), layer, jnp.int32)
        h = _ffn(l, h, *weights["f1"])
        rows, cols, c = _proj(l, h, *weights["proj"], batch)
        o_sb = _sb_attention(rows, cols, tri, batch, k_sb_col, q_sb_row, v_sb_row)
        c = c.reshape(batch, nq, t, LANES)
        c_ends = c[:, :, t - 1, :N_FOX_HEADS]
        k_max = jnp.sqrt(jnp.max(c[:, :, :, KNORM_LANE:KNORM_LANE + N_FOX_HEADS], axis=(1, 2)))
        c_ends = jnp.concatenate([c_ends, k_max[:, None, :]], axis=1)
        c_ends = c_ends.reshape(batch, nq + 1, N_FOX_HEADS // 2, 2).transpose(0, 2, 1, 3)
        o_fx = _fox_attention(rows, cols, c_ends, batch)
        h = _post(l, h, rows, o_sb, o_fx, kvmem, *weights["post"], batch, qmem_col)
        h = _ffn(l, h, *weights["f2"])
    return h.reshape(batch, seq, d)
```

```python
import math

import jax
import jax.numpy as jnp
from jax import lax
from jax.experimental import pallas as pl
from jax.experimental.pallas import tpu as pltpu

F32 = jnp.float32
BF16 = jnp.bfloat16

HEAD_DIM = 64
N_SB_HEADS = 8
N_FOX_HEADS = 8
N_MEM_HEADS = 4
MEM_HEAD_DIM = 128
N_BRANCH = 3
RMS_EPS = 1e-6
LOG2_E = math.log2(math.e)

LANES = 128
MXU_WIDTH = 256
VMEM_LIMIT_BYTES = 56 * 1024 * 1024
ROW_TILE = 512
WIDE_ROW_TILE = 1024
FFN_CHUNKS = 2
ATT_TILE = 256
FOX_SUB_TILES = 4
SB_SUB_TILES = 2
KNORM_LANE = 32
EXP_DEAD = 93.0
EXP2_DEAD = 135.0
SOFTPLUS2_CLAMP = 86.0
NORM_SLACK = 1.01
MASKED = -1e30

_NT = (((1,), (1,)), ((), ()))


def _rms(x, g):
    return x * lax.rsqrt(jnp.mean(x * x, axis=-1, keepdims=True) + RMS_EPS) * g


def _params(*sem):
    return pltpu.CompilerParams(dimension_semantics=sem, vmem_limit_bytes=VMEM_LIMIT_BYTES)


def _resident(shape, index_map):
    return pl.BlockSpec(shape, index_map, pipeline_mode=pl.Buffered(1))


def _ffn_kernel(l_ref, h_ref, gpre_ref, gpost_ref, wg_ref, wu_ref, wd_ref, o_ref):
    x = h_ref[...]
    u = _rms(x, gpre_ref[...]).astype(BF16)
    n_mxu = wg_ref.shape[1] // MXU_WIDTH
    cuts = [MXU_WIDTH * ((n_mxu * k + FFN_CHUNKS - 1) // FFN_CHUNKS) for k in range(FFN_CHUNKS + 1)]
    f = None
    for k in range(FFN_CHUNKS):
        cols = slice(cuts[k], cuts[k + 1])
        g = jnp.dot(u, wg_ref[:, cols], preferred_element_type=F32)
        up = jnp.dot(u, wu_ref[:, cols], preferred_element_type=F32)
        a = (g * jax.nn.sigmoid(g) * up).astype(BF16)
        part = jnp.dot(a, wd_ref[cols, :], preferred_element_type=F32)
        f = part if f is None else f + part
    o_ref[...] = x + 0.5 * _rms(f, gpost_ref[...])


def _ffn(l, h, g_pre, g_post, w_gate, w_up, w_down):
    m, d = h.shape
    f = w_gate.shape[-1]
    tm = WIDE_ROW_TILE
    wmap = lambda i, l: (l[0], 0, 0)
    return pl.pallas_call(
        _ffn_kernel,
        grid_spec=pltpu.PrefetchScalarGridSpec(
            num_scalar_prefetch=1,
            grid=(m // tm,),
            in_specs=[
                pl.BlockSpec((tm, d), lambda i, l: (i, 0)),
                _resident((None, 1, d), wmap),
                _resident((None, 1, d), wmap),
                _resident((None, d, f), wmap),
                _resident((None, d, f), wmap),
                _resident((None, f, d), wmap),
            ],
            out_specs=pl.BlockSpec((tm, d), lambda i, l: (i, 0)),
        ),
        out_shape=jax.ShapeDtypeStruct((m, d), F32),
        compiler_params=_params("parallel"),
        name="ffn",
    )(l, h, g_pre, g_post, w_gate, w_up, w_down)


def _split3(x):
    hi = x.astype(BF16).astype(F32)
    mid = (x - hi).astype(BF16).astype(F32)
    lo = ((x - hi) - mid).astype(BF16).astype(F32)
    return hi, mid, lo


def _fold3(x, nh):
    n = x.shape[1]
    return x + pltpu.roll(x, n - nh, axis=1) + pltpu.roll(x, n - 2 * nh, axis=1)


def _spread3(terms, nh):
    lane = lax.broadcasted_iota(jnp.int32, terms[0].shape, 1)
    out = jnp.where(lane < nh, terms[0], 0.0)
    for i in (1, 2):
        moved = pltpu.roll(terms[i], i * nh, axis=1)
        out = jnp.where(jnp.logical_and(lane >= i * nh, lane < (i + 1) * nh), moved, out)
    return out.astype(BF16)


def _proj_kernel(l_ref, h_ref, g_ref, w_ref, wt_ref, wg_ref, bf_ref, lower_ref, place_ref,
                 rows_ref, cols_ref, c_ref, carry_ref):
    @pl.when(pl.program_id(1) == 0)
    def _():
        carry_ref[...] = jnp.zeros_like(carry_ref)

    nh = N_FOX_HEADS
    u = _rms(h_ref[...], g_ref[...])
    u_hi, u_mid, u_lo = _split3(u)
    ub = u_hi.astype(BF16)

    w_gate3 = wg_ref[...]
    e_hi = jnp.dot(ub, w_gate3, preferred_element_type=F32)
    e_mid = jnp.dot(u_mid.astype(BF16), w_gate3, preferred_element_type=F32)
    e_lo = jnp.dot(u_lo.astype(BF16), w_gate3, preferred_element_type=F32)
    drop_lo = lax.broadcasted_iota(jnp.int32, e_mid.shape, 1) < 2 * nh
    fl = _fold3(e_hi + jnp.where(drop_lo, e_mid, 0.0), nh) + e_lo + bf_ref[...]
    log_f = jnp.minimum(fl, 0.0) - jnp.log1p(jnp.exp(-jnp.abs(fl)))

    sums = jnp.dot(lower_ref[...], _spread3(_split3(log_f), nh), preferred_element_type=F32)
    c = _fold3(sums, nh) + carry_ref[0:1, :]
    tm = c.shape[0]
    carry_ref[...] = jnp.broadcast_to(c[tm - 1:tm, :], carry_ref.shape)
    placed = jnp.dot(_spread3(_split3(c), nh), place_ref[...], preferred_element_type=F32)

    main = jnp.dot(ub, w_ref[...], preferred_element_type=F32)
    n_k = N_FOX_HEADS * HEAD_DIM
    kf = main[:, :n_k]
    dim = lax.broadcasted_iota(jnp.int32, (n_k, LANES), 0)
    lane_k = lax.broadcasted_iota(jnp.int32, (n_k, LANES), 1)
    head_of_dim = jnp.where(lane_k == KNORM_LANE + dim // HEAD_DIM, 1.0, 0.0).astype(BF16)
    norms2 = jnp.dot((kf * kf).astype(BF16), head_of_dim, preferred_element_type=F32)
    lane_c = lax.broadcasted_iota(jnp.int32, c.shape, 1)
    c_ref[...] = jnp.where(jnp.logical_and(lane_c >= KNORM_LANE, lane_c < KNORM_LANE + nh),
                           norms2, c)
    pair_w = 3 * LANES
    for p in range(n_k // LANES):
        base = p * pair_w
        rows_ref[:, base:base + LANES] = main[:, p * LANES:(p + 1) * LANES].astype(BF16)
        rows_ref[:, base + LANES:base + pair_w] = (
            placed[:, 2 * p * LANES:2 * (p + 1) * LANES].astype(BF16))
    rows_ref[:, (n_k // LANES) * pair_w:] = main[:, n_k:].astype(BF16)

    across = lax.dot_general(wt_ref[...], ub, _NT, preferred_element_type=F32).astype(BF16)
    t = ATT_TILE
    for s in range(cols_ref.shape[0]):
        cols_ref[s] = across[:, s * t:(s + 1) * t]


def _proj(l, h, g, w_main, w_t, w_gate3, b_f, lower, place, batch):
    m, d = h.shape
    seq = m // batch
    tm = ROW_TILE
    nt = seq // tm
    per = tm // ATT_TILE
    n_k = N_FOX_HEADS * HEAD_DIM
    n_rows_out = w_main.shape[-1] - n_k + (n_k // LANES) * 3 * LANES
    n_cols_out = w_t.shape[1]
    wmap = lambda b, i, l: (l[0], 0, 0)
    const = lambda b, i, l: (0, 0)
    return pl.pallas_call(
        _proj_kernel,
        grid_spec=pltpu.PrefetchScalarGridSpec(
            num_scalar_prefetch=1,
            grid=(batch, nt),
            in_specs=[
                pl.BlockSpec((tm, d), lambda b, i, l: (b * nt + i, 0)),
                _resident((None, 1, d), wmap),
                _resident((None, d, w_main.shape[-1]), wmap),
                _resident((None, n_cols_out, d), wmap),
                _resident((None, d, LANES), wmap),
                _resident((None, 1, LANES), wmap),
                _resident(lower.shape, const),
                _resident(place.shape, const),
            ],
            out_specs=[
                pl.BlockSpec((tm, n_rows_out), lambda b, i, l: (b * nt + i, 0)),
                pl.BlockSpec((None, per, n_cols_out, ATT_TILE), lambda b, i, l: (b, i, 0, 0)),
                pl.BlockSpec((tm, LANES), lambda b, i, l: (b * nt + i, 0)),
            ],
            scratch_shapes=[pltpu.VMEM((8, LANES), F32)],
        ),
        out_shape=[
            jax.ShapeDtypeStruct((m, n_rows_out), BF16),
            jax.ShapeDtypeStruct((batch, seq // ATT_TILE, n_cols_out, ATT_TILE), BF16),
            jax.ShapeDtypeStruct((m, LANES), F32),
        ],
        compiler_params=_params("parallel", "arbitrary"),
        name="proj",
    )(l, h, g, w_main, w_t, w_gate3, b_f, lower, place)


def _softplus2(z2):
    return jnp.maximum(z2, jnp.log2(1.0 + jnp.exp2(jnp.minimum(z2, SOFTPLUS2_CLAMP))))


def _sb_kernel(qt_ref, k_ref, vt_ref, tri_ref, o_ref, acc_ref):
    t = ATT_TILE
    qi = pl.program_id(2)
    sub = lax.broadcasted_iota(jnp.int32, (LANES, t), 0)
    own = (sub < HEAD_DIM, sub >= HEAD_DIM)
    n_sub = qt_ref.shape[0]
    chains = [(s, hh) for s in range(n_sub) for hh in range(2)]
    n_ch = len(chains)
    diag = [n_sub * qi + s for s in range(n_sub)]
    q_heads = []
    for s, hh in chains:
        q_pair = qt_ref[s]
        q_heads.append(jnp.where(own[hh], q_pair, jnp.zeros_like(q_pair)))
    tri = tri_ref[...]

    key = lax.broadcasted_iota(jnp.int32, (t, t), 0)
    query = lax.broadcasted_iota(jnp.int32, (t, t), 1)
    strictly_causal = key < query

    def scores(c, j, mask):
        kb = k_ref[pl.ds(pl.multiple_of(j * t, t), t), :]
        z = jnp.dot(kb, q_heads[c], preferred_element_type=F32)
        return z if mask is None else jnp.where(mask, z, MASKED)

    def decay(z):
        sp = _softplus2(z)
        later = jnp.dot(tri, sp.astype(BF16), preferred_element_type=F32)
        return sp, later, jnp.sum(sp, axis=0, keepdims=True)

    def weigh(z, sp, later, carry, j):
        w = jnp.exp2((z - sp) - later - carry)
        return jnp.dot(vt_ref[j], w.astype(BF16), preferred_element_type=F32)

    prev = [jnp.maximum(diag[s] - 1, 0) for s, hh in chains]
    zd = [scores(c, diag[s], strictly_causal) for c, (s, hh) in enumerate(chains)]
    zp = [scores(c, prev[c], None) for c in range(n_ch)]
    dd = [decay(z) for z in zd]
    dp = [decay(z) for z in zp]
    pvd = [weigh(zd[c], dd[c][0], dd[c][1], 0.0, diag[s]) for c, (s, hh) in enumerate(chains)]
    pvp = [weigh(zp[c], dp[c][0], dp[c][1], dd[c][2], prev[c]) for c in range(n_ch)]
    carries = []
    for c, (s, hh) in enumerate(chains):
        acc_ref[c] = pvd[c] + jnp.where(diag[s] > 0, pvp[c], 0.0)
        carries.append(dd[c][2] + dp[c][2])

    def alive(r_next, cs):
        more = [jnp.logical_and(diag[s] - r_next >= 0, jnp.min(cs[c]) < EXP2_DEAD)
                for c, (s, hh) in enumerate(chains)]
        any_more = more[0]
        for m in more[1:]:
            any_more = jnp.logical_or(any_more, m)
        return any_more.astype(jnp.int32)

    def cond(state):
        return state[1] > 0

    def body(state):
        r, cs = state[0], state[2:]
        js = [diag[s] - r for s, hh in chains]
        zs = [scores(c, jnp.maximum(js[c], 0), None) for c in range(n_ch)]
        ds = [decay(z) for z in zs]
        new = []
        for c in range(n_ch):
            pv = weigh(zs[c], ds[c][0], ds[c][1], cs[c], jnp.maximum(js[c], 0))
            acc_ref[c] += jnp.where(js[c] >= 0, pv, 0.0)
            new.append(cs[c] + ds[c][2])
        return (r + 1, alive(r + 1, new)) + tuple(new)

    lax.while_loop(cond, body, (jnp.int32(2), alive(2, carries)) + tuple(carries))
    for s in range(n_sub):
        o_ref[s * t:(s + 1) * t, :] = jnp.where(
            own[0], acc_ref[2 * s], acc_ref[2 * s + 1]).T.astype(o_ref.dtype)


def _sb_attention(rows, cols, tri, batch, k_col, q_row, v_row):
    m = rows.shape[0]
    seq = m // batch
    t = ATT_TILE
    nq = seq // t
    n_sub = SB_SUB_TILES
    steps = nq // n_sub
    n_pairs = N_SB_HEADS * HEAD_DIM // LANES
    return pl.pallas_call(
        _sb_kernel,
        grid=(batch, n_pairs, steps),
        in_specs=[
            pl.BlockSpec((None, n_sub, LANES, t), lambda b, p, i: (b, i, q_row + p, 0)),
            pl.BlockSpec((seq, LANES), lambda b, p, i: (b, k_col + p)),
            pl.BlockSpec((None, nq, LANES, t), lambda b, p, i: (b, 0, v_row + p, 0)),
            _resident(tri.shape, lambda b, p, i: (0, 0)),
        ],
        out_specs=pl.BlockSpec((n_sub * t, LANES), lambda b, p, i: (b * steps + i, p)),
        out_shape=jax.ShapeDtypeStruct((m, n_pairs * LANES), BF16),
        scratch_shapes=[pltpu.VMEM((2 * n_sub, LANES, t), F32)],
        compiler_params=_params("parallel", "parallel", "arbitrary"),
        name="sb_attn",
    )(cols, rows, cols, tri)


def _fox_kernel(qt_ref, k_ref, vt_ref, cend_ref, o_ref, y_ref, p_ref, acc_ref):
    t = ATT_TILE
    qi = pl.program_id(2)
    sub = lax.broadcasted_iota(jnp.int32, (LANES, t), 0)
    own = (sub < HEAD_DIM, sub >= HEAD_DIM)
    n_tiles = cend_ref.shape[0] - 1
    k_max = cend_ref[n_tiles:n_tiles + 1, :]

    n_sub = qt_ref.shape[0]
    chains = [(s, hh) for s in range(n_sub) for hh in range(2)]
    n_ch = len(chains)
    diag = [n_sub * qi + s for s in range(n_sub)]
    minus_c = jnp.where(sub < 3, -1.0, 0.0).astype(BF16)
    q_aug, z_bound = [], []
    for s, hh in chains:
        q_pair = qt_ref[s]
        qf = q_pair.astype(F32)
        z_bound.append(jnp.sqrt(jnp.sum(jnp.where(own[hh], qf * qf, 0.0), axis=0, keepdims=True))
                       * (k_max[:, hh:hh + 1] * NORM_SLACK))
        q_aug.append(jnp.concatenate([jnp.where(own[hh], q_pair, jnp.zeros_like(q_pair)), minus_c],
                                     axis=0))
    one = jnp.ones((LANES, t), BF16)

    def logits(c, j):
        kb = k_ref[pl.ds(pl.multiple_of(j * t, t), t), :]
        lhs = kb[:, :2 * LANES] if chains[c][1] == 0 else jnp.concatenate(
            [kb[:, :LANES], kb[:, 2 * LANES:]], axis=1)
        return jnp.dot(lhs, q_aug[c], preferred_element_type=F32)

    def softmax_step(y, m_prev, skip=None):
        m_new = jnp.maximum(m_prev, jnp.max(y, axis=0, keepdims=True))
        shift = m_new
        if skip is not None:
            m_new = jnp.where(skip, m_prev, m_new)
            shift = jnp.where(skip, -MASKED, m_new)
        return jnp.exp(y - shift).astype(BF16), jnp.exp(m_prev - m_new), m_new

    def weighted_values(c, j):
        vt = jnp.where(own[chains[c][1]], vt_ref[j], one)
        return jnp.dot(vt, p_ref[c], preferred_element_type=F32)

    def alive(r_next, ms):
        done = []
        for c, (s, hh) in enumerate(chains):
            j = diag[s] - r_next
            far = cend_ref[pl.ds(jnp.maximum(j, 0), 1), :][:, hh:hh + 1]
            dead = jnp.max(z_bound[c] - far - ms[c]) < -EXP_DEAD
            done.append(jnp.logical_or(j < 0, dead))
        all_done = done[0]
        for d in done[1:]:
            all_done = jnp.logical_and(all_done, d)
        return jnp.logical_not(all_done).astype(jnp.int32)

    key = lax.broadcasted_iota(jnp.int32, (t, t), 0)
    query = lax.broadcasted_iota(jnp.int32, (t, t), 1)
    causal = key <= query
    yd = [logits(c, diag[s]) for c, (s, hh) in enumerate(chains)]
    yn = [logits(c, jnp.maximum(diag[s] - 1, 0)) for c, (s, hh) in enumerate(chains)]
    ms, alphas = [], []
    for c in range(n_ch):
        p, _, m_new = softmax_step(jnp.where(causal, yd[c], MASKED), jnp.full((1, t), MASKED, F32))
        p_ref[c] = p
        y_ref[c] = yn[c]
        acc_ref[c] = jnp.zeros((LANES, t), F32)
        ms.append(m_new)
        alphas.append(jnp.ones((1, t), F32))

    def cond(state):
        r, live = state[0], state[1]
        return jnp.logical_and(r <= diag[n_sub - 1], live > 0)

    def body(state):
        r = state[0]
        ms, alphas = state[2:2 + n_ch], state[2 + n_ch:]
        live = alive(r + 1, ms)
        pv = [weighted_values(c, jnp.maximum(diag[s] - r + 1, 0)) for c, (s, hh) in enumerate(chains)]
        yn = [logits(c, jnp.maximum(diag[s] - r - 1, 0)) for c, (s, hh) in enumerate(chains)]
        new_m, new_a = [], []
        for c, (s, hh) in enumerate(chains):
            p, alpha, m_new = softmax_step(y_ref[c], ms[c],
                                           diag[s] - r < 0 if s < n_sub - 1 else None)
            acc_ref[c] = alphas[c] * acc_ref[c] + pv[c]
            p_ref[c] = p
            y_ref[c] = yn[c]
            new_m.append(m_new)
            new_a.append(alpha)
        return (r + 1, live) + tuple(new_m) + tuple(new_a)

    state = lax.while_loop(cond, body, (jnp.int32(1), alive(1, ms)) + tuple(ms) + tuple(alphas))
    r_end = state[0]
    for s in range(n_sub):
        outs = []
        for hh in range(2):
            c = 2 * s + hh
            acc = state[2 + n_ch + c] * acc_ref[c] + weighted_values(
                c, jnp.maximum(diag[s] - r_end + 1, 0))
            outs.append(acc / pltpu.roll(acc, HEAD_DIM, axis=0))
        o_ref[s * t:(s + 1) * t, :] = jnp.where(own[0], outs[0], outs[1]).T.astype(o_ref.dtype)


def _fox_attention(rows, cols, c_ends, batch):
    m = rows.shape[0]
    seq = m // batch
    t = ATT_TILE
    nq = seq // t
    n_pairs = N_FOX_HEADS * HEAD_DIM // LANES
    n_sub = FOX_SUB_TILES
    steps = nq // n_sub
    n_ch = 2 * n_sub
    return pl.pallas_call(
        _fox_kernel,
        grid=(batch, n_pairs, steps),
        in_specs=[
            pl.BlockSpec((None, n_sub, LANES, t), lambda b, p, i: (b, i, p, 0)),
            pl.BlockSpec((seq, 3 * LANES), lambda b, p, i: (b, p)),
            pl.BlockSpec((None, nq, LANES, t), lambda b, p, i: (b, 0, n_pairs + p, 0)),
            pl.BlockSpec((None, None, nq + 1, 2), lambda b, p, i: (b, p, 0, 0)),
        ],
        out_specs=pl.BlockSpec((n_sub * t, LANES), lambda b, p, i: (b * steps + i, p)),
        out_shape=jax.ShapeDtypeStruct((m, n_pairs * LANES), BF16),
        scratch_shapes=[
            pltpu.VMEM((n_ch, t, t), F32),
            pltpu.VMEM((n_ch, t, t), BF16),
            pltpu.VMEM((n_ch, LANES, t), F32),
        ],
        compiler_params=_params("parallel", "parallel", "arbitrary"),
        name="fox_attn",
    )(cols, rows, cols, c_ends)


def _memkv_kernel(mem_ref, g_ref, w_ref, o_ref):
    mem_n = _rms(mem_ref[...], g_ref[...]).astype(BF16)
    o_ref[...] = jnp.dot(mem_n, w_ref[...], preferred_element_type=F32).astype(BF16)


def _memkv(mem2d, g, w_mem_kv):
    n_layers, d, n_out = w_mem_kv.shape
    rows = mem2d.shape[0]
    return pl.pallas_call(
        _memkv_kernel,
        grid=(n_layers,),
        in_specs=[
            _resident((rows, d), lambda l: (0, 0)),
            _resident((1, d), lambda l: (0, 0)),
            pl.BlockSpec((None, d, n_out), lambda l: (l, 0, 0)),
        ],
        out_specs=pl.BlockSpec((None, rows, n_out), lambda l: (l, 0, 0)),
        out_shape=jax.ShapeDtypeStruct((n_layers, rows, n_out), BF16),
        compiler_params=_params("parallel"),
        name="memkv",
    )(mem2d, g, w_mem_kv)


def _post_kernel(l_ref, h_ref, qmem_ref, osb_ref, ofx_ref, kv_ref, gpre_ref, gpost_ref,
                 wgate_ref, bgate_ref, wsb_ref, wfx_ref, wmem_ref, wout_ref, o_ref):
    x = h_ref[...]
    d = x.shape[1]
    u = _rms(x, gpre_ref[...]).astype(BF16)
    gates = jax.nn.sigmoid(jnp.dot(u, wgate_ref[...], preferred_element_type=F32) + bgate_ref[...])

    qm = qmem_ref[...]
    kv = kv_ref[...]
    mem_w = N_MEM_HEADS * MEM_HEAD_DIM
    scale = MEM_HEAD_DIM ** -0.5
    heads = []
    for hh in range(N_MEM_HEADS):
        lo, hi = hh * MEM_HEAD_DIM, (hh + 1) * MEM_HEAD_DIM
        z = lax.dot_general(qm[:, lo:hi], kv[:, lo:hi], _NT, preferred_element_type=F32) * scale
        e = jnp.exp(z - jnp.max(z, axis=1, keepdims=True))
        p = (e / jnp.sum(e, axis=1, keepdims=True)).astype(BF16)
        heads.append(jnp.dot(p, kv[:, mem_w + lo:mem_w + hi], preferred_element_type=F32))
    o_mem = jnp.concatenate(heads, axis=1).astype(BF16)

    merged = (gates[:, 0:d] * jnp.dot(osb_ref[...], wsb_ref[...], preferred_element_type=F32)
              + gates[:, d:2 * d] * jnp.dot(ofx_ref[...], wfx_ref[...], preferred_element_type=F32)
              + gates[:, 2 * d:3 * d] * jnp.dot(o_mem, wmem_ref[...], preferred_element_type=F32))
    y = jnp.dot(merged.astype(BF16), wout_ref[...], preferred_element_type=F32)
    o_ref[...] = x + _rms(y, gpost_ref[...])


def _post(l, h, qkv, o_sb, o_fx, kvmem, g_pre, g_post, w_gate, b_gate, w_sb, w_fx, w_mem, w_out,
          batch, qmem_col):
    m, d = h.shape
    seq = m // batch
    tm = WIDE_ROW_TILE
    nt = seq // tm
    mem_len = kvmem.shape[1] // batch
    mem_w = N_MEM_HEADS * MEM_HEAD_DIM
    wmap = lambda b, i, l: (l[0], 0, 0)
    rows = lambda b, i, l: (b * nt + i, 0)
    return pl.pallas_call(
        _post_kernel,
        grid_spec=pltpu.PrefetchScalarGridSpec(
            num_scalar_prefetch=1,
            grid=(batch, nt),
            in_specs=[
                pl.BlockSpec((tm, d), rows),
                pl.BlockSpec((tm, mem_w), lambda b, i, l: (b * nt + i, qmem_col)),
                pl.BlockSpec((tm, o_sb.shape[1]), rows),
                pl.BlockSpec((tm, o_fx.shape[1]), rows),
                pl.BlockSpec((None, mem_len, 2 * mem_w), lambda b, i, l: (l[0], b, 0)),
                _resident((None, 1, d), wmap),
                _resident((None, 1, d), wmap),
                _resident((None, d, N_BRANCH * d), wmap),
                _resident((None, 1, N_BRANCH * d), wmap),
                _resident((None,) + w_sb.shape[1:], wmap),
                _resident((None,) + w_fx.shape[1:], wmap),
                _resident((None,) + w_mem.shape[1:], wmap),
                _resident((None, d, d), wmap),
            ],
            out_specs=pl.BlockSpec((tm, d), rows),
        ),
        out_shape=jax.ShapeDtypeStruct((m, d), F32),
        compiler_params=_params("parallel", "parallel"),
        name="post",
    )(l, h, qkv, o_sb, o_fx, kvmem, g_pre, g_post, w_gate, b_gate, w_sb, w_fx, w_mem, w_out)


def kernel(x, mem, ffn1_pre_g, ffn1_post_g, ffn1_w_gate, ffn1_w_up, ffn1_w_down, mix_pre_g, mix_post_g, w_in, b_forget, mem_norm_g, w_mem_kv, w_gate, b_gate, w_br_sb, w_br_fox, w_br_mem, w_out, ffn2_pre_g, ffn2_post_g, ffn2_w_gate, ffn2_w_up, ffn2_w_down):
    batch, seq, d = x.shape
    n_layers = w_in.shape[0]
    sb_w = N_SB_HEADS * HEAD_DIM
    fox_w = N_FOX_HEADS * HEAD_DIM
    mem_w = N_MEM_HEADS * MEM_HEAD_DIM
    assert seq % WIDE_ROW_TILE == 0 and WIDE_ROW_TILE % ROW_TILE == 0 and d % LANES == 0
    assert seq % (max(FOX_SUB_TILES, SB_SUB_TILES) * ATT_TILE) == 0 and ROW_TILE % ATT_TILE == 0
    assert w_in.shape[2] == 3 * sb_w + 3 * fox_w + N_FOX_HEADS + mem_w

    scale = HEAD_DIM ** -0.5
    fx = 3 * sb_w
    f_lo = fx + 3 * fox_w
    f_hi = f_lo + N_FOX_HEADS
    bf = lambda w: w.astype(BF16)
    vec = lambda g: g[:, None, :]

    order = jnp.argsort(b_forget, axis=1)

    def by_head(w, axis):
        shape = [1] * w.ndim
        shape[0], shape[axis] = order.shape[0], order.shape[1]
        return jnp.take_along_axis(w, order.reshape(shape), axis=axis)

    def fox_cols(lo):
        w = w_in[:, :, lo:lo + fox_w].reshape(n_layers, d, N_FOX_HEADS, HEAD_DIM)
        return by_head(w, 2).reshape(n_layers, d, fox_w)

    w_gate_f = by_head(w_in[:, :, f_lo:f_hi], 2)
    g_hi = w_gate_f.astype(BF16)
    g_mid = (w_gate_f - g_hi.astype(F32)).astype(BF16)
    g_lo = ((w_gate_f - g_hi.astype(F32)) - g_mid.astype(F32)).astype(BF16)
    gate3 = jnp.pad(jnp.concatenate([g_hi, g_mid, g_lo], axis=2),
                    ((0, 0), (0, 0), (0, LANES - 3 * N_FOX_HEADS)))
    w_main = jnp.concatenate([bf(fox_cols(fx + fox_w)), bf(w_in[:, :, sb_w:2 * sb_w]),
                              bf(w_in[:, :, f_hi:])], axis=2)
    w_t = jnp.swapaxes(jnp.concatenate(
        [fox_cols(fx) * scale, fox_cols(fx + 2 * fox_w),
         w_in[:, :, :sb_w] * (scale * LOG2_E), w_in[:, :, 2 * sb_w:fx]], axis=2), 1, 2).astype(BF16)
    b_f = jnp.pad(by_head(b_forget, 1), ((0, 0), (0, LANES - N_FOX_HEADS)))[:, None, :]
    w_br_fox_sorted = by_head(w_br_fox.reshape(n_layers, N_FOX_HEADS, HEAD_DIM, d), 1).reshape(
        w_br_fox.shape)
    term, head = jnp.meshgrid(jnp.arange(3), jnp.arange(N_FOX_HEADS), indexing="ij")
    place = jnp.zeros((LANES, N_FOX_HEADS * LANES), BF16).at[
        term * N_FOX_HEADS + head, head * LANES + term].set(1.0)
    tok = jnp.arange(ROW_TILE)
    lower = (tok[:, None] >= tok[None, :]).astype(BF16)
    weights = dict(
        f1=(vec(ffn1_pre_g), vec(ffn1_post_g), bf(ffn1_w_gate), bf(ffn1_w_up), bf(ffn1_w_down)),
        f2=(vec(ffn2_pre_g), vec(ffn2_post_g), bf(ffn2_w_gate), bf(ffn2_w_up), bf(ffn2_w_down)),
        proj=(vec(mix_pre_g), w_main, w_t, gate3, b_f, lower, place),
        post=(vec(mix_pre_g), vec(mix_post_g), bf(w_gate), vec(b_gate), bf(w_br_sb),
              bf(w_br_fox_sorted), bf(w_br_mem), bf(w_out)),
    )

    t = ATT_TILE
    tri = (jnp.arange(t)[None, :] > jnp.arange(t)[:, None]).astype(BF16)

    kvmem = _memkv(mem.reshape(batch * mem.shape[1], d), mem_norm_g[None, :], bf(w_mem_kv))

    fox_k_w = (fox_w // LANES) * 3 * LANES
    k_sb_col = fox_k_w // LANES
    qmem_col = (fox_k_w + sb_w) // mem_w
    q_sb_row = 2 * fox_w // LANES
    v_sb_row = q_sb_row + sb_w // LANES
    nq = seq // t

    h = x.reshape(batch * seq, d)
    for layer in range(n_layers):
        l = jnp.full((1,), layer, jnp.int32)
        h = _ffn(l, h, *weights["f1"])
        rows, cols, c = _proj(l, h, *weights["proj"], batch)
        o_sb = _sb_attention(rows, cols, tri, batch, k_sb_col, q_sb_row, v_sb_row)
        c = c.reshape(batch, nq, t, LANES)
        c_ends = c[:, :, t - 1, :N_FOX_HEADS]
        k_max = jnp.sqrt(jnp.max(c[:, :, :, KNORM_LANE:KNORM_LANE + N_FOX_HEADS], axis=(1, 2)))
        c_ends = jnp.concatenate([c_ends, k_max[:, None, :]], axis=1)
        c_ends = c_ends.reshape(batch, nq + 1, N_FOX_HEADS // 2, 2).transpose(0, 2, 1, 3)
        o_fx = _fox_attention(rows, cols, c_ends, batch)
        h = _post(l, h, rows, o_sb, o_fx, kvmem, *weights["post"], batch, qmem_col)
        h = _ffn(l, h, *weights["f2"])
    return h.reshape(batch, seq, d)
```

```python
import math

import jax
import jax.numpy as jnp
from jax import lax
from jax.experimental import pallas as pl
from jax.experimental.pallas import tpu as pltpu

F32 = jnp.float32
BF16 = jnp.bfloat16

HEAD_DIM = 64
N_SB_HEADS = 8
N_FOX_HEADS = 8
N_MEM_HEADS = 4
MEM_HEAD_DIM = 128
N_BRANCH = 3
RMS_EPS = 1e-6
LOG2_E = math.log2(math.e)

LANES = 128
MXU_WIDTH = 256
VMEM_LIMIT_BYTES = 56 * 1024 * 1024
ROW_TILE = 512
WIDE_ROW_TILE = 1024
FFN_CHUNKS = 2
ATT_TILE = 256
FOX_SUB_TILES = 4
SB_SUB_TILES = 4
KNORM_LANE = 32
EXP_DEAD = 93.0
EXP2_DEAD = 135.0
SOFTPLUS2_CLAMP = 86.0
NORM_SLACK = 1.01
MASKED = -1e30

_NT = (((1,), (1,)), ((), ()))


def _rms(x, g):
    return x * lax.rsqrt(jnp.mean(x * x, axis=-1, keepdims=True) + RMS_EPS) * g


def _params(*sem):
    return pltpu.CompilerParams(dimension_semantics=sem, vmem_limit_bytes=VMEM_LIMIT_BYTES)


def _resident(shape, index_map):
    return pl.BlockSpec(shape, index_map, pipeline_mode=pl.Buffered(1))


def _ffn_kernel(l_ref, h_ref, gpre_ref, gpost_ref, wg_ref, wu_ref, wd_ref, o_ref):
    x = h_ref[...]
    u = _rms(x, gpre_ref[...]).astype(BF16)
    n_mxu = wg_ref.shape[1] // MXU_WIDTH
    cuts = [MXU_WIDTH * ((n_mxu * k + FFN_CHUNKS - 1) // FFN_CHUNKS) for k in range(FFN_CHUNKS + 1)]
    f = None
    for k in range(FFN_CHUNKS):
        cols = slice(cuts[k], cuts[k + 1])
        g = jnp.dot(u, wg_ref[:, cols], preferred_element_type=F32)
        up = jnp.dot(u, wu_ref[:, cols], preferred_element_type=F32)
        a = (g * jax.nn.sigmoid(g) * up).astype(BF16)
        part = jnp.dot(a, wd_ref[cols, :], preferred_element_type=F32)
        f = part if f is None else f + part
    o_ref[...] = x + 0.5 * _rms(f, gpost_ref[...])


def _ffn(l, h, g_pre, g_post, w_gate, w_up, w_down):
    m, d = h.shape
    f = w_gate.shape[-1]
    tm = WIDE_ROW_TILE
    wmap = lambda i, l: (l[0], 0, 0)
    return pl.pallas_call(
        _ffn_kernel,
        grid_spec=pltpu.PrefetchScalarGridSpec(
            num_scalar_prefetch=1,
            grid=(m // tm,),
            in_specs=[
                pl.BlockSpec((tm, d), lambda i, l: (i, 0)),
                _resident((None, 1, d), wmap),
                _resident((None, 1, d), wmap),
                _resident((None, d, f), wmap),
                _resident((None, d, f), wmap),
                _resident((None, f, d), wmap),
            ],
            out_specs=pl.BlockSpec((tm, d), lambda i, l: (i, 0)),
        ),
        out_shape=jax.ShapeDtypeStruct((m, d), F32),
        compiler_params=_params("parallel"),
        name="ffn",
    )(l, h, g_pre, g_post, w_gate, w_up, w_down)


def _split3(x):
    hi = x.astype(BF16).astype(F32)
    mid = (x - hi).astype(BF16).astype(F32)
    lo = ((x - hi) - mid).astype(BF16).astype(F32)
    return hi, mid, lo


def _fold3(x, nh):
    n = x.shape[1]
    return x + pltpu.roll(x, n - nh, axis=1) + pltpu.roll(x, n - 2 * nh, axis=1)


def _spread3(terms, nh):
    lane = lax.broadcasted_iota(jnp.int32, terms[0].shape, 1)
    out = jnp.where(lane < nh, terms[0], 0.0)
    for i in (1, 2):
        moved = pltpu.roll(terms[i], i * nh, axis=1)
        out = jnp.where(jnp.logical_and(lane >= i * nh, lane < (i + 1) * nh), moved, out)
    return out.astype(BF16)


def _proj_kernel(l_ref, h_ref, g_ref, w_ref, wt_ref, wg_ref, bf_ref, lower_ref, place_ref,
                 rows_ref, cols_ref, c_ref, carry_ref):
    @pl.when(pl.program_id(1) == 0)
    def _():
        carry_ref[...] = jnp.zeros_like(carry_ref)

    nh = N_FOX_HEADS
    u = _rms(h_ref[...], g_ref[...])
    u_hi, u_mid, u_lo = _split3(u)
    ub = u_hi.astype(BF16)

    w_gate3 = wg_ref[...]
    e_hi = jnp.dot(ub, w_gate3, preferred_element_type=F32)
    e_mid = jnp.dot(u_mid.astype(BF16), w_gate3, preferred_element_type=F32)
    e_lo = jnp.dot(u_lo.astype(BF16), w_gate3, preferred_element_type=F32)
    drop_lo = lax.broadcasted_iota(jnp.int32, e_mid.shape, 1) < 2 * nh
    fl = _fold3(e_hi + jnp.where(drop_lo, e_mid, 0.0), nh) + e_lo + bf_ref[...]
    log_f = jnp.minimum(fl, 0.0) - jnp.log1p(jnp.exp(-jnp.abs(fl)))

    sums = jnp.dot(lower_ref[...], _spread3(_split3(log_f), nh), preferred_element_type=F32)
    c = _fold3(sums, nh) + carry_ref[0:1, :]
    tm = c.shape[0]
    carry_ref[...] = jnp.broadcast_to(c[tm - 1:tm, :], carry_ref.shape)
    placed = jnp.dot(_spread3(_split3(c), nh), place_ref[...], preferred_element_type=F32)

    main = jnp.dot(ub, w_ref[...], preferred_element_type=F32)
    n_k = N_FOX_HEADS * HEAD_DIM
    kf = main[:, :n_k]
    dim = lax.broadcasted_iota(jnp.int32, (n_k, LANES), 0)
    lane_k = lax.broadcasted_iota(jnp.int32, (n_k, LANES), 1)
    head_of_dim = jnp.where(lane_k == KNORM_LANE + dim // HEAD_DIM, 1.0, 0.0).astype(BF16)
    norms2 = jnp.dot((kf * kf).astype(BF16), head_of_dim, preferred_element_type=F32)
    lane_c = lax.broadcasted_iota(jnp.int32, c.shape, 1)
    c_ref[...] = jnp.where(jnp.logical_and(lane_c >= KNORM_LANE, lane_c < KNORM_LANE + nh),
                           norms2, c)
    pair_w = 3 * LANES
    for p in range(n_k // LANES):
        base = p * pair_w
        rows_ref[:, base:base + LANES] = main[:, p * LANES:(p + 1) * LANES].astype(BF16)
        rows_ref[:, base + LANES:base + pair_w] = (
            placed[:, 2 * p * LANES:2 * (p + 1) * LANES].astype(BF16))
    rows_ref[:, (n_k // LANES) * pair_w:] = main[:, n_k:].astype(BF16)

    across = lax.dot_general(wt_ref[...], ub, _NT, preferred_element_type=F32).astype(BF16)
    t = ATT_TILE
    for s in range(cols_ref.shape[0]):
        cols_ref[s] = across[:, s * t:(s + 1) * t]


def _proj(l, h, g, w_main, w_t, w_gate3, b_f, lower, place, batch):
    m, d = h.shape
    seq = m // batch
    tm = ROW_TILE
    nt = seq // tm
    per = tm // ATT_TILE
    n_k = N_FOX_HEADS * HEAD_DIM
    n_rows_out = w_main.shape[-1] - n_k + (n_k // LANES) * 3 * LANES
    n_cols_out = w_t.shape[1]
    wmap = lambda b, i, l: (l[0], 0, 0)
    const = lambda b, i, l: (0, 0)
    return pl.pallas_call(
        _proj_kernel,
        grid_spec=pltpu.PrefetchScalarGridSpec(
            num_scalar_prefetch=1,
            grid=(batch, nt),
            in_specs=[
                pl.BlockSpec((tm, d), lambda b, i, l: (b * nt + i, 0)),
                _resident((None, 1, d), wmap),
                _resident((None, d, w_main.shape[-1]), wmap),
                _resident((None, n_cols_out, d), wmap),
                _resident((None, d, LANES), wmap),
                _resident((None, 1, LANES), wmap),
                _resident(lower.shape, const),
                _resident(place.shape, const),
            ],
            out_specs=[
                pl.BlockSpec((tm, n_rows_out), lambda b, i, l: (b * nt + i, 0)),
                pl.BlockSpec((None, per, n_cols_out, ATT_TILE), lambda b, i, l: (b, i, 0, 0)),
                pl.BlockSpec((tm, LANES), lambda b, i, l: (b * nt + i, 0)),
            ],
            scratch_shapes=[pltpu.VMEM((8, LANES), F32)],
        ),
        out_shape=[
            jax.ShapeDtypeStruct((m, n_rows_out), BF16),
            jax.ShapeDtypeStruct((batch, seq // ATT_TILE, n_cols_out, ATT_TILE), BF16),
            jax.ShapeDtypeStruct((m, LANES), F32),
        ],
        compiler_params=_params("parallel", "arbitrary"),
        name="proj",
    )(l, h, g, w_main, w_t, w_gate3, b_f, lower, place)


def _softplus2(z2):
    return jnp.maximum(z2, jnp.log2(1.0 + jnp.exp2(jnp.minimum(z2, SOFTPLUS2_CLAMP))))


def _sb_kernel(qt_ref, k_ref, vt_ref, tri_ref, o_ref, acc_ref):
    t = ATT_TILE
    qi = pl.program_id(2)
    sub = lax.broadcasted_iota(jnp.int32, (LANES, t), 0)
    own = (sub < HEAD_DIM, sub >= HEAD_DIM)
    n_sub = qt_ref.shape[0]
    chains = [(s, hh) for s in range(n_sub) for hh in range(2)]
    n_ch = len(chains)
    diag = [n_sub * qi + s for s in range(n_sub)]
    q_heads = []
    for s, hh in chains:
        q_pair = qt_ref[s]
        q_heads.append(jnp.where(own[hh], q_pair, jnp.zeros_like(q_pair)))
    tri = tri_ref[...]

    key = lax.broadcasted_iota(jnp.int32, (t, t), 0)
    query = lax.broadcasted_iota(jnp.int32, (t, t), 1)
    strictly_causal = key < query

    def scores(c, j, mask):
        kb = k_ref[pl.ds(pl.multiple_of(j * t, t), t), :]
        z = jnp.dot(kb, q_heads[c], preferred_element_type=F32)
        return z if mask is None else jnp.where(mask, z, MASKED)

    def decay(z):
        sp = _softplus2(z)
        later = jnp.dot(tri, sp.astype(BF16), preferred_element_type=F32)
        return sp, later, jnp.sum(sp, axis=0, keepdims=True)

    def weigh(z, sp, later, carry, j):
        w = jnp.exp2((z - sp) - later - carry)
        return jnp.dot(vt_ref[j], w.astype(BF16), preferred_element_type=F32)

    prev = [jnp.maximum(diag[s] - 1, 0) for s, hh in chains]
    zd = [scores(c, diag[s], strictly_causal) for c, (s, hh) in enumerate(chains)]
    zp = [scores(c, prev[c], None) for c in range(n_ch)]
    dd = [decay(z) for z in zd]
    dp = [decay(z) for z in zp]
    pvd = [weigh(zd[c], dd[c][0], dd[c][1], 0.0, diag[s]) for c, (s, hh) in enumerate(chains)]
    pvp = [weigh(zp[c], dp[c][0], dp[c][1], dd[c][2], prev[c]) for c in range(n_ch)]
    carries = []
    for c, (s, hh) in enumerate(chains):
        acc_ref[c] = pvd[c] + jnp.where(diag[s] > 0, pvp[c], 0.0)
        carries.append(dd[c][2] + dp[c][2])

    def alive(r_next, cs):
        more = [jnp.logical_and(diag[s] - r_next >= 0, jnp.min(cs[c]) < EXP2_DEAD)
                for c, (s, hh) in enumerate(chains)]
        any_more = more[0]
        for m in more[1:]:
            any_more = jnp.logical_or(any_more, m)
        return any_more.astype(jnp.int32)

    def cond(state):
        return state[1] > 0

    def body(state):
        r, cs = state[0], state[2:]
        js = [diag[s] - r for s, hh in chains]
        zs = [scores(c, jnp.maximum(js[c], 0), None) for c in range(n_ch)]
        ds = [decay(z) for z in zs]
        new = []
        for c in range(n_ch):
            pv = weigh(zs[c], ds[c][0], ds[c][1], cs[c], jnp.maximum(js[c], 0))
            acc_ref[c] += jnp.where(js[c] >= 0, pv, 0.0)
            new.append(cs[c] + ds[c][2])
        return (r + 1, alive(r + 1, new)) + tuple(new)

    lax.while_loop(cond, body, (jnp.int32(2), alive(2, carries)) + tuple(carries))
    for s in range(n_sub):
        o_ref[s * t:(s + 1) * t, :] = jnp.where(
            own[0], acc_ref[2 * s], acc_ref[2 * s + 1]).T.astype(o_ref.dtype)


def _sb_attention(rows, cols, tri, batch, k_col, q_row, v_row):
    m = rows.shape[0]
    seq = m // batch
    t = ATT_TILE
    nq = seq // t
    n_sub = SB_SUB_TILES
    steps = nq // n_sub
    n_pairs = N_SB_HEADS * HEAD_DIM // LANES
    return pl.pallas_call(
        _sb_kernel,
        grid=(batch, n_pairs, steps),
        in_specs=[
            pl.BlockSpec((None, n_sub, LANES, t), lambda b, p, i: (b, i, q_row + p, 0)),
            pl.BlockSpec((seq, LANES), lambda b, p, i: (b, k_col + p)),
            pl.BlockSpec((None, nq, LANES, t), lambda b, p, i: (b, 0, v_row + p, 0)),
            _resident(tri.shape, lambda b, p, i: (0, 0)),
        ],
        out_specs=pl.BlockSpec((n_sub * t, LANES), lambda b, p, i: (b * steps + i, p)),
        out_shape=jax.ShapeDtypeStruct((m, n_pairs * LANES), BF16),
        scratch_shapes=[pltpu.VMEM((2 * n_sub, LANES, t), F32)],
        compiler_params=_params("parallel", "parallel", "arbitrary"),
        name="sb_attn",
    )(cols, rows, cols, tri)


def _fox_kernel(qt_ref, k_ref, vt_ref, cend_ref, o_ref, y_ref, p_ref, acc_ref):
    t = ATT_TILE
    qi = pl.program_id(2)
    sub = lax.broadcasted_iota(jnp.int32, (LANES, t), 0)
    own = (sub < HEAD_DIM, sub >= HEAD_DIM)
    n_tiles = cend_ref.shape[0] - 1
    k_max = cend_ref[n_tiles:n_tiles + 1, :]

    n_sub = qt_ref.shape[0]
    chains = [(s, hh) for s in range(n_sub) for hh in range(2)]
    n_ch = len(chains)
    diag = [n_sub * qi + s for s in range(n_sub)]
    minus_c = jnp.where(sub < 3, -1.0, 0.0).astype(BF16)
    q_aug, z_bound = [], []
    for s, hh in chains:
        q_pair = qt_ref[s]
        qf = q_pair.astype(F32)
        z_bound.append(jnp.sqrt(jnp.sum(jnp.where(own[hh], qf * qf, 0.0), axis=0, keepdims=True))
                       * (k_max[:, hh:hh + 1] * NORM_SLACK))
        q_aug.append(jnp.concatenate([jnp.where(own[hh], q_pair, jnp.zeros_like(q_pair)), minus_c],
                                     axis=0))
    one = jnp.ones((LANES, t), BF16)

    def logits(c, j):
        kb = k_ref[pl.ds(pl.multiple_of(j * t, t), t), :]
        lhs = kb[:, :2 * LANES] if chains[c][1] == 0 else jnp.concatenate(
            [kb[:, :LANES], kb[:, 2 * LANES:]], axis=1)
        return jnp.dot(lhs, q_aug[c], preferred_element_type=F32)

    def softmax_step(y, m_prev, skip=None):
        m_new = jnp.maximum(m_prev, jnp.max(y, axis=0, keepdims=True))
        shift = m_new
        if skip is not None:
            m_new = jnp.where(skip, m_prev, m_new)
            shift = jnp.where(skip, -MASKED, m_new)
        return jnp.exp(y - shift).astype(BF16), jnp.exp(m_prev - m_new), m_new

    def weighted_values(c, j):
        vt = jnp.where(own[chains[c][1]], vt_ref[j], one)
        return jnp.dot(vt, p_ref[c], preferred_element_type=F32)

    def alive(r_next, ms):
        done = []
        for c, (s, hh) in enumerate(chains):
            j = diag[s] - r_next
            far = cend_ref[pl.ds(jnp.maximum(j, 0), 1), :][:, hh:hh + 1]
            dead = jnp.max(z_bound[c] - far - ms[c]) < -EXP_DEAD
            done.append(jnp.logical_or(j < 0, dead))
        all_done = done[0]
        for d in done[1:]:
            all_done = jnp.logical_and(all_done, d)
        return jnp.logical_not(all_done).astype(jnp.int32)

    key = lax.broadcasted_iota(jnp.int32, (t, t), 0)
    query = lax.broadcasted_iota(jnp.int32, (t, t), 1)
    causal = key <= query
    yd = [logits(c, diag[s]) for c, (s, hh) in enumerate(chains)]
    yn = [logits(c, jnp.maximum(diag[s] - 1, 0)) for c, (s, hh) in enumerate(chains)]
    ms, alphas = [], []
    for c in range(n_ch):
        p, _, m_new = softmax_step(jnp.where(causal, yd[c], MASKED), jnp.full((1, t), MASKED, F32))
        p_ref[c] = p
        y_ref[c] = yn[c]
        acc_ref[c] = jnp.zeros((LANES, t), F32)
        ms.append(m_new)
        alphas.append(jnp.ones((1, t), F32))

    def cond(state):
        r, live = state[0], state[1]
        return jnp.logical_and(r <= diag[n_sub - 1], live > 0)

    def body(state):
        r = state[0]
        ms, alphas = state[2:2 + n_ch], state[2 + n_ch:]
        live = alive(r + 1, ms)
        pv = [weighted_values(c, jnp.maximum(diag[s] - r + 1, 0)) for c, (s, hh) in enumerate(chains)]
        yn = [logits(c, jnp.maximum(diag[s] - r - 1, 0)) for c, (s, hh) in enumerate(chains)]
        new_m, new_a = [], []
        for c, (s, hh) in enumerate(chains):
            p, alpha, m_new = softmax_step(y_ref[c], ms[c],
                                           diag[s] - r < 0 if s < n_sub - 1 else None)
            acc_ref[c] = alphas[c] * acc_ref[c] + pv[c]
            p_ref[c] = p
            y_ref[c] = yn[c]
            new_m.append(m_new)
            new_a.append(alpha)
        return (r + 1, live) + tuple(new_m) + tuple(new_a)

    state = lax.while_loop(cond, body, (jnp.int32(1), alive(1, ms)) + tuple(ms) + tuple(alphas))
    r_end = state[0]
    for s in range(n_sub):
        outs = []
        for hh in range(2):
            c = 2 * s + hh
            acc = state[2 + n_ch + c] * acc_ref[c] + weighted_values(
                c, jnp.maximum(diag[s] - r_end + 1, 0))
            outs.append(acc / pltpu.roll(acc, HEAD_DIM, axis=0))
        o_ref[s * t:(s + 1) * t, :] = jnp.where(own[0], outs[0], outs[1]).T.astype(o_ref.dtype)


def _fox_attention(rows, cols, c_ends, batch):
    m = rows.shape[0]
    seq = m // batch
    t = ATT_TILE
    nq = seq // t
    n_pairs = N_FOX_HEADS * HEAD_DIM // LANES
    n_sub = FOX_SUB_TILES
    steps = nq // n_sub
    n_ch = 2 * n_sub
    return pl.pallas_call(
        _fox_kernel,
        grid=(batch, n_pairs, steps),
        in_specs=[
            pl.BlockSpec((None, n_sub, LANES, t), lambda b, p, i: (b, i, p, 0)),
            pl.BlockSpec((seq, 3 * LANES), lambda b, p, i: (b, p)),
            pl.BlockSpec((None, nq, LANES, t), lambda b, p, i: (b, 0, n_pairs + p, 0)),
            pl.BlockSpec((None, None, nq + 1, 2), lambda b, p, i: (b, p, 0, 0)),
        ],
        out_specs=pl.BlockSpec((n_sub * t, LANES), lambda b, p, i: (b * steps + i, p)),
        out_shape=jax.ShapeDtypeStruct((m, n_pairs * LANES), BF16),
        scratch_shapes=[
            pltpu.VMEM((n_ch, t, t), F32),
            pltpu.VMEM((n_ch, t, t), BF16),
            pltpu.VMEM((n_ch, LANES, t), F32),
        ],
        compiler_params=_params("parallel", "parallel", "arbitrary"),
        name="fox_attn",
    )(cols, rows, cols, c_ends)


def _memkv_kernel(mem_ref, g_ref, w_ref, o_ref):
    mem_n = _rms(mem_ref[...], g_ref[...]).astype(BF16)
    o_ref[...] = jnp.dot(mem_n, w_ref[...], preferred_element_type=F32).astype(BF16)


def _memkv(mem2d, g, w_mem_kv):
    n_layers, d, n_out = w_mem_kv.shape
    rows = mem2d.shape[0]
    return pl.pallas_call(
        _memkv_kernel,
        grid=(n_layers,),
        in_specs=[
            _resident((rows, d), lambda l: (0, 0)),
            _resident((1, d), lambda l: (0, 0)),
            pl.BlockSpec((None, d, n_out), lambda l: (l, 0, 0)),
        ],
        out_specs=pl.BlockSpec((None, rows, n_out), lambda l: (l, 0, 0)),
        out_shape=jax.ShapeDtypeStruct((n_layers, rows, n_out), BF16),
        compiler_params=_params("parallel"),
        name="memkv",
    )(mem2d, g, w_mem_kv)


def _post_kernel(l_ref, h_ref, qmem_ref, osb_ref, ofx_ref, kv_ref, gpre_ref, gpost_ref,
                 wgate_ref, bgate_ref, wsb_ref, wfx_ref, wmem_ref, wout_ref, o_ref):
    x = h_ref[...]
    d = x.shape[1]
    u = _rms(x, gpre_ref[...]).astype(BF16)
    gates = jax.nn.sigmoid(jnp.dot(u, wgate_ref[...], preferred_element_type=F32) + bgate_ref[...])

    qm = qmem_ref[...]
    kv = kv_ref[...]
    mem_w = N_MEM_HEADS * MEM_HEAD_DIM
    scale = MEM_HEAD_DIM ** -0.5
    heads = []
    for hh in range(N_MEM_HEADS):
        lo, hi = hh * MEM_HEAD_DIM, (hh + 1) * MEM_HEAD_DIM
        z = lax.dot_general(qm[:, lo:hi], kv[:, lo:hi], _NT, preferred_element_type=F32) * scale
        e = jnp.exp(z - jnp.max(z, axis=1, keepdims=True))
        p = (e / jnp.sum(e, axis=1, keepdims=True)).astype(BF16)
        heads.append(jnp.dot(p, kv[:, mem_w + lo:mem_w + hi], preferred_element_type=F32))
    o_mem = jnp.concatenate(heads, axis=1).astype(BF16)

    merged = (gates[:, 0:d] * jnp.dot(osb_ref[...], wsb_ref[...], preferred_element_type=F32)
              + gates[:, d:2 * d] * jnp.dot(ofx_ref[...], wfx_ref[...], preferred_element_type=F32)
              + gates[:, 2 * d:3 * d] * jnp.dot(o_mem, wmem_ref[...], preferred_element_type=F32))
    y = jnp.dot(merged.astype(BF16), wout_ref[...], preferred_element_type=F32)
    o_ref[...] = x + _rms(y, gpost_ref[...])


def _post(l, h, qkv, o_sb, o_fx, kvmem, g_pre, g_post, w_gate, b_gate, w_sb, w_fx, w_mem, w_out,
          batch, qmem_col):
    m, d = h.shape
    seq = m // batch
    tm = WIDE_ROW_TILE
    nt = seq // tm
    mem_len = kvmem.shape[1] // batch
    mem_w = N_MEM_HEADS * MEM_HEAD_DIM
    wmap = lambda b, i, l: (l[0], 0, 0)
    rows = lambda b, i, l: (b * nt + i, 0)
    return pl.pallas_call(
        _post_kernel,
        grid_spec=pltpu.PrefetchScalarGridSpec(
            num_scalar_prefetch=1,
            grid=(batch, nt),
            in_specs=[
                pl.BlockSpec((tm, d), rows),
                pl.BlockSpec((tm, mem_w), lambda b, i, l: (b * nt + i, qmem_col)),
                pl.BlockSpec((tm, o_sb.shape[1]), rows),
                pl.BlockSpec((tm, o_fx.shape[1]), rows),
                pl.BlockSpec((None, mem_len, 2 * mem_w), lambda b, i, l: (l[0], b, 0)),
                _resident((None, 1, d), wmap),
                _resident((None, 1, d), wmap),
                _resident((None, d, N_BRANCH * d), wmap),
                _resident((None, 1, N_BRANCH * d), wmap),
                _resident((None,) + w_sb.shape[1:], wmap),
                _resident((None,) + w_fx.shape[1:], wmap),
                _resident((None,) + w_mem.shape[1:], wmap),
                _resident((None, d, d), wmap),
            ],
            out_specs=pl.BlockSpec((tm, d), rows),
        ),
        out_shape=jax.ShapeDtypeStruct((m, d), F32),
        compiler_params=_params("parallel", "parallel"),
        name="post",
    )(l, h, qkv, o_sb, o_fx, kvmem, g_pre, g_post, w_gate, b_gate, w_sb, w_fx, w_mem, w_out)


def kernel(x, mem, ffn1_pre_g, ffn1_post_g, ffn1_w_gate, ffn1_w_up, ffn1_w_down, mix_pre_g, mix_post_g, w_in, b_forget, mem_norm_g, w_mem_kv, w_gate, b_gate, w_br_sb, w_br_fox, w_br_mem, w_out, ffn2_pre_g, ffn2_post_g, ffn2_w_gate, ffn2_w_up, ffn2_w_down):
    batch, seq, d = x.shape
    n_layers = w_in.shape[0]
    sb_w = N_SB_HEADS * HEAD_DIM
    fox_w = N_FOX_HEADS * HEAD_DIM
    mem_w = N_MEM_HEADS * MEM_HEAD_DIM
    assert seq % WIDE_ROW_TILE == 0 and WIDE_ROW_TILE % ROW_TILE == 0 and d % LANES == 0
    assert seq % (max(FOX_SUB_TILES, SB_SUB_TILES) * ATT_TILE) == 0 and ROW_TILE % ATT_TILE == 0
    assert w_in.shape[2] == 3 * sb_w + 3 * fox_w + N_FOX_HEADS + mem_w

    scale = HEAD_DIM ** -0.5
    fx = 3 * sb_w
    f_lo = fx + 3 * fox_w
    f_hi = f_lo + N_FOX_HEADS
    bf = lambda w: w.astype(BF16)
    vec = lambda g: g[:, None, :]

    order = jnp.argsort(b_forget, axis=1)

    def by_head(w, axis):
        shape = [1] * w.ndim
        shape[0], shape[axis] = order.shape[0], order.shape[1]
        return jnp.take_along_axis(w, order.reshape(shape), axis=axis)

    def fox_cols(lo):
        w = w_in[:, :, lo:lo + fox_w].reshape(n_layers, d, N_FOX_HEADS, HEAD_DIM)
        return by_head(w, 2).reshape(n_layers, d, fox_w)

    w_gate_f = by_head(w_in[:, :, f_lo:f_hi], 2)
    g_hi = w_gate_f.astype(BF16)
    g_mid = (w_gate_f - g_hi.astype(F32)).astype(BF16)
    g_lo = ((w_gate_f - g_hi.astype(F32)) - g_mid.astype(F32)).astype(BF16)
    gate3 = jnp.pad(jnp.concatenate([g_hi, g_mid, g_lo], axis=2),
                    ((0, 0), (0, 0), (0, LANES - 3 * N_FOX_HEADS)))
    w_main = jnp.concatenate([bf(fox_cols(fx + fox_w)), bf(w_in[:, :, sb_w:2 * sb_w]),
                              bf(w_in[:, :, f_hi:])], axis=2)
    w_t = jnp.swapaxes(jnp.concatenate(
        [fox_cols(fx) * scale, fox_cols(fx + 2 * fox_w),
         w_in[:, :, :sb_w] * (scale * LOG2_E), w_in[:, :, 2 * sb_w:fx]], axis=2), 1, 2).astype(BF16)
    b_f = jnp.pad(by_head(b_forget, 1), ((0, 0), (0, LANES - N_FOX_HEADS)))[:, None, :]
    w_br_fox_sorted = by_head(w_br_fox.reshape(n_layers, N_FOX_HEADS, HEAD_DIM, d), 1).reshape(
        w_br_fox.shape)
    term, head = jnp.meshgrid(jnp.arange(3), jnp.arange(N_FOX_HEADS), indexing="ij")
    place = jnp.zeros((LANES, N_FOX_HEADS * LANES), BF16).at[
        term * N_FOX_HEADS + head, head * LANES + term].set(1.0)
    tok = jnp.arange(ROW_TILE)
    lower = (tok[:, None] >= tok[None, :]).astype(BF16)
    weights = dict(
        f1=(vec(ffn1_pre_g), vec(ffn1_post_g), bf(ffn1_w_gate), bf(ffn1_w_up), bf(ffn1_w_down)),
        f2=(vec(ffn2_pre_g), vec(ffn2_post_g), bf(ffn2_w_gate), bf(ffn2_w_up), bf(ffn2_w_down)),
        proj=(vec(mix_pre_g), w_main, w_t, gate3, b_f, lower, place),
        post=(vec(mix_pre_g), vec(mix_post_g), bf(w_gate), vec(b_gate), bf(w_br_sb),
              bf(w_br_fox_sorted), bf(w_br_mem), bf(w_out)),
    )

    t = ATT_TILE
    tri = (jnp.arange(t)[None, :] > jnp.arange(t)[:, None]).astype(BF16)

    kvmem = _memkv(mem.reshape(batch * mem.shape[1], d), mem_norm_g[None, :], bf(w_mem_kv))

    fox_k_w = (fox_w // LANES) * 3 * LANES
    k_sb_col = fox_k_w // LANES
    qmem_col = (fox_k_w + sb_w) // mem_w
    q_sb_row = 2 * fox_w // LANES
    v_sb_row = q_sb_row + sb_w // LANES
    nq = seq // t

    h = x.reshape(batch * seq, d)
    for layer in range(n_layers):
        l = jnp.full((1,), layer, jnp.int32)
        h = _ffn(l, h, *weights["f1"])
        rows, cols, c = _proj(l, h, *weights["proj"], batch)
        o_sb = _sb_attention(rows, cols, tri, batch, k_sb_col, q_sb_row, v_sb_row)
        c = c.reshape(batch, nq, t, LANES)
        c_ends = c[:, :, t - 1, :N_FOX_HEADS]
        k_max = jnp.sqrt(jnp.max(c[:, :, :, KNORM_LANE:KNORM_LANE + N_FOX_HEADS], axis=(1, 2)))
        c_ends = jnp.concatenate([c_ends, k_max[:, None, :]], axis=1)
        c_ends = c_ends.reshape(batch, nq + 1, N_FOX_HEADS // 2, 2).transpose(0, 2, 1, 3)
        o_fx = _fox_attention(rows, cols, c_ends, batch)
        h = _post(l, h, rows, o_sb, o_fx, kvmem, *weights["post"], batch, qmem_col)
        h = _ffn(l, h, *weights["f2"])
    return h.reshape(batch, seq, d)
```

```python
import math

import jax
import jax.numpy as jnp
from jax import lax
from jax.experimental import pallas as pl
from jax.experimental.pallas import tpu as pltpu

F32 = jnp.float32
BF16 = jnp.bfloat16

HEAD_DIM = 64
N_SB_HEADS = 8
N_FOX_HEADS = 8
N_MEM_HEADS = 4
MEM_HEAD_DIM = 128
N_BRANCH = 3
RMS_EPS = 1e-6
LOG2_E = math.log2(math.e)

LANES = 128
MXU_WIDTH = 256
VMEM_LIMIT_BYTES = 56 * 1024 * 1024
ROW_TILE = 512
WIDE_ROW_TILE = 1024
FFN_CHUNKS = 2
ATT_TILE = 256
FOX_SUB_TILES = 4
SB_SUB_TILES = 8
KNORM_LANE = 32
EXP_DEAD = 93.0
EXP2_DEAD = 135.0
SOFTPLUS2_CLAMP = 86.0
NORM_SLACK = 1.01
MASKED = -1e30

_NT = (((1,), (1,)), ((), ()))


def _rms(x, g):
    return x * lax.rsqrt(jnp.mean(x * x, axis=-1, keepdims=True) + RMS_EPS) * g


def _params(*sem):
    return pltpu.CompilerParams(dimension_semantics=sem, vmem_limit_bytes=VMEM_LIMIT_BYTES)


def _resident(shape, index_map):
    return pl.BlockSpec(shape, index_map, pipeline_mode=pl.Buffered(1))


def _ffn_kernel(l_ref, h_ref, gpre_ref, gpost_ref, wg_ref, wu_ref, wd_ref, o_ref):
    x = h_ref[...]
    u = _rms(x, gpre_ref[...]).astype(BF16)
    n_mxu = wg_ref.shape[1] // MXU_WIDTH
    cuts = [MXU_WIDTH * ((n_mxu * k + FFN_CHUNKS - 1) // FFN_CHUNKS) for k in range(FFN_CHUNKS + 1)]
    f = None
    for k in range(FFN_CHUNKS):
        cols = slice(cuts[k], cuts[k + 1])
        g = jnp.dot(u, wg_ref[:, cols], preferred_element_type=F32)
        up = jnp.dot(u, wu_ref[:, cols], preferred_element_type=F32)
        a = (g * jax.nn.sigmoid(g) * up).astype(BF16)
        part = jnp.dot(a, wd_ref[cols, :], preferred_element_type=F32)
        f = part if f is None else f + part
    o_ref[...] = x + 0.5 * _rms(f, gpost_ref[...])


def _ffn(l, h, g_pre, g_post, w_gate, w_up, w_down):
    m, d = h.shape
    f = w_gate.shape[-1]
    tm = WIDE_ROW_TILE
    wmap = lambda i, l: (l[0], 0, 0)
    return pl.pallas_call(
        _ffn_kernel,
        grid_spec=pltpu.PrefetchScalarGridSpec(
            num_scalar_prefetch=1,
            grid=(m // tm,),
            in_specs=[
                pl.BlockSpec((tm, d), lambda i, l: (i, 0)),
                _resident((None, 1, d), wmap),
                _resident((None, 1, d), wmap),
                _resident((None, d, f), wmap),
                _resident((None, d, f), wmap),
                _resident((None, f, d), wmap),
            ],
            out_specs=pl.BlockSpec((tm, d), lambda i, l: (i, 0)),
        ),
        out_shape=jax.ShapeDtypeStruct((m, d), F32),
        compiler_params=_params("parallel"),
        name="ffn",
    )(l, h, g_pre, g_post, w_gate, w_up, w_down)


def _split3(x):
    hi = x.astype(BF16).astype(F32)
    mid = (x - hi).astype(BF16).astype(F32)
    lo = ((x - hi) - mid).astype(BF16).astype(F32)
    return hi, mid, lo


def _fold3(x, nh):
    n = x.shape[1]
    return x + pltpu.roll(x, n - nh, axis=1) + pltpu.roll(x, n - 2 * nh, axis=1)


def _spread3(terms, nh):
    lane = lax.broadcasted_iota(jnp.int32, terms[0].shape, 1)
    out = jnp.where(lane < nh, terms[0], 0.0)
    for i in (1, 2):
        moved = pltpu.roll(terms[i], i * nh, axis=1)
        out = jnp.where(jnp.logical_and(lane >= i * nh, lane < (i + 1) * nh), moved, out)
    return out.astype(BF16)


def _proj_kernel(l_ref, h_ref, g_ref, w_ref, wt_ref, wg_ref, bf_ref, lower_ref, place_ref,
                 rows_ref, cols_ref, c_ref, carry_ref):
    @pl.when(pl.program_id(1) == 0)
    def _():
        carry_ref[...] = jnp.zeros_like(carry_ref)

    nh = N_FOX_HEADS
    u = _rms(h_ref[...], g_ref[...])
    u_hi, u_mid, u_lo = _split3(u)
    ub = u_hi.astype(BF16)

    w_gate3 = wg_ref[...]
    e_hi = jnp.dot(ub, w_gate3, preferred_element_type=F32)
    e_mid = jnp.dot(u_mid.astype(BF16), w_gate3, preferred_element_type=F32)
    e_lo = jnp.dot(u_lo.astype(BF16), w_gate3, preferred_element_type=F32)
    drop_lo = lax.broadcasted_iota(jnp.int32, e_mid.shape, 1) < 2 * nh
    fl = _fold3(e_hi + jnp.where(drop_lo, e_mid, 0.0), nh) + e_lo + bf_ref[...]
    log_f = jnp.minimum(fl, 0.0) - jnp.log1p(jnp.exp(-jnp.abs(fl)))

    sums = jnp.dot(lower_ref[...], _spread3(_split3(log_f), nh), preferred_element_type=F32)
    c = _fold3(sums, nh) + carry_ref[0:1, :]
    tm = c.shape[0]
    carry_ref[...] = jnp.broadcast_to(c[tm - 1:tm, :], carry_ref.shape)
    placed = jnp.dot(_spread3(_split3(c), nh), place_ref[...], preferred_element_type=F32)

    main = jnp.dot(ub, w_ref[...], preferred_element_type=F32)
    n_k = N_FOX_HEADS * HEAD_DIM
    kf = main[:, :n_k]
    dim = lax.broadcasted_iota(jnp.int32, (n_k, LANES), 0)
    lane_k = lax.broadcasted_iota(jnp.int32, (n_k, LANES), 1)
    head_of_dim = jnp.where(lane_k == KNORM_LANE + dim // HEAD_DIM, 1.0, 0.0).astype(BF16)
    norms2 = jnp.dot((kf * kf).astype(BF16), head_of_dim, preferred_element_type=F32)
    lane_c = lax.broadcasted_iota(jnp.int32, c.shape, 1)
    c_ref[...] = jnp.where(jnp.logical_and(lane_c >= KNORM_LANE, lane_c < KNORM_LANE + nh),
                           norms2, c)
    pair_w = 3 * LANES
    for p in range(n_k // LANES):
        base = p * pair_w
        rows_ref[:, base:base + LANES] = main[:, p * LANES:(p + 1) * LANES].astype(BF16)
        rows_ref[:, base + LANES:base + pair_w] = (
            placed[:, 2 * p * LANES:2 * (p + 1) * LANES].astype(BF16))
    rows_ref[:, (n_k // LANES) * pair_w:] = main[:, n_k:].astype(BF16)

    across = lax.dot_general(wt_ref[...], ub, _NT, preferred_element_type=F32).astype(BF16)
    t = ATT_TILE
    for s in range(cols_ref.shape[0]):
        cols_ref[s] = across[:, s * t:(s + 1) * t]


def _proj(l, h, g, w_main, w_t, w_gate3, b_f, lower, place, batch):
    m, d = h.shape
    seq = m // batch
    tm = ROW_TILE
    nt = seq // tm
    per = tm // ATT_TILE
    n_k = N_FOX_HEADS * HEAD_DIM
    n_rows_out = w_main.shape[-1] - n_k + (n_k // LANES) * 3 * LANES
    n_cols_out = w_t.shape[1]
    wmap = lambda b, i, l: (l[0], 0, 0)
    const = lambda b, i, l: (0, 0)
    return pl.pallas_call(
        _proj_kernel,
        grid_spec=pltpu.PrefetchScalarGridSpec(
            num_scalar_prefetch=1,
            grid=(batch, nt),
            in_specs=[
                pl.BlockSpec((tm, d), lambda b, i, l: (b * nt + i, 0)),
                _resident((None, 1, d), wmap),
                _resident((None, d, w_main.shape[-1]), wmap),
                _resident((None, n_cols_out, d), wmap),
                _resident((None, d, LANES), wmap),
                _resident((None, 1, LANES), wmap),
                _resident(lower.shape, const),
                _resident(place.shape, const),
            ],
            out_specs=[
                pl.BlockSpec((tm, n_rows_out), lambda b, i, l: (b * nt + i, 0)),
                pl.BlockSpec((None, per, n_cols_out, ATT_TILE), lambda b, i, l: (b, i, 0, 0)),
                pl.BlockSpec((tm, LANES), lambda b, i, l: (b * nt + i, 0)),
            ],
            scratch_shapes=[pltpu.VMEM((8, LANES), F32)],
        ),
        out_shape=[
            jax.ShapeDtypeStruct((m, n_rows_out), BF16),
            jax.ShapeDtypeStruct((batch, seq // ATT_TILE, n_cols_out, ATT_TILE), BF16),
            jax.ShapeDtypeStruct((m, LANES), F32),
        ],
        compiler_params=_params("parallel", "arbitrary"),
        name="proj",
    )(l, h, g, w_main, w_t, w_gate3, b_f, lower, place)


def _softplus2(z2):
    return jnp.maximum(z2, jnp.log2(1.0 + jnp.exp2(jnp.minimum(z2, SOFTPLUS2_CLAMP))))


def _sb_kernel(qt_ref, k_ref, vt_ref, tri_ref, o_ref, acc_ref):
    t = ATT_TILE
    qi = pl.program_id(2)
    sub = lax.broadcasted_iota(jnp.int32, (LANES, t), 0)
    own = (sub < HEAD_DIM, sub >= HEAD_DIM)
    n_sub = qt_ref.shape[0]
    chains = [(s, hh) for s in range(n_sub) for hh in range(2)]
    n_ch = len(chains)
    diag = [n_sub * qi + s for s in range(n_sub)]
    q_heads = []
    for s, hh in chains:
        q_pair = qt_ref[s]
        q_heads.append(jnp.where(own[hh], q_pair, jnp.zeros_like(q_pair)))
    tri = tri_ref[...]

    key = lax.broadcasted_iota(jnp.int32, (t, t), 0)
    query = lax.broadcasted_iota(jnp.int32, (t, t), 1)
    strictly_causal = key < query

    def scores(c, j, mask):
        kb = k_ref[pl.ds(pl.multiple_of(j * t, t), t), :]
        z = jnp.dot(kb, q_heads[c], preferred_element_type=F32)
        return z if mask is None else jnp.where(mask, z, MASKED)

    def decay(z):
        sp = _softplus2(z)
        later = jnp.dot(tri, sp.astype(BF16), preferred_element_type=F32)
        return sp, later, jnp.sum(sp, axis=0, keepdims=True)

    def weigh(z, sp, later, carry, j):
        w = jnp.exp2((z - sp) - later - carry)
        return jnp.dot(vt_ref[j], w.astype(BF16), preferred_element_type=F32)

    prev = [jnp.maximum(diag[s] - 1, 0) for s, hh in chains]
    zd = [scores(c, diag[s], strictly_causal) for c, (s, hh) in enumerate(chains)]
    zp = [scores(c, prev[c], None) for c in range(n_ch)]
    dd = [decay(z) for z in zd]
    dp = [decay(z) for z in zp]
    pvd = [weigh(zd[c], dd[c][0], dd[c][1], 0.0, diag[s]) for c, (s, hh) in enumerate(chains)]
    pvp = [weigh(zp[c], dp[c][0], dp[c][1], dd[c][2], prev[c]) for c in range(n_ch)]
    carries = []
    for c, (s, hh) in enumerate(chains):
        acc_ref[c] = pvd[c] + jnp.where(diag[s] > 0, pvp[c], 0.0)
        carries.append(dd[c][2] + dp[c][2])

    def alive(r_next, cs):
        more = [jnp.logical_and(diag[s] - r_next >= 0, jnp.min(cs[c]) < EXP2_DEAD)
                for c, (s, hh) in enumerate(chains)]
        any_more = more[0]
        for m in more[1:]:
            any_more = jnp.logical_or(any_more, m)
        return any_more.astype(jnp.int32)

    def cond(state):
        return state[1] > 0

    def body(state):
        r, cs = state[0], state[2:]
        js = [diag[s] - r for s, hh in chains]
        zs = [scores(c, jnp.maximum(js[c], 0), None) for c in range(n_ch)]
        ds = [decay(z) for z in zs]
        new = []
        for c in range(n_ch):
            pv = weigh(zs[c], ds[c][0], ds[c][1], cs[c], jnp.maximum(js[c], 0))
            acc_ref[c] += jnp.where(js[c] >= 0, pv, 0.0)
            new.append(cs[c] + ds[c][2])
        return (r + 1, alive(r + 1, new)) + tuple(new)

    lax.while_loop(cond, body, (jnp.int32(2), alive(2, carries)) + tuple(carries))
    for s in range(n_sub):
        o_ref[s * t:(s + 1) * t, :] = jnp.where(
            own[0], acc_ref[2 * s], acc_ref[2 * s + 1]).T.astype(o_ref.dtype)


def _sb_attention(rows, cols, tri, batch, k_col, q_row, v_row):
    m = rows.shape[0]
    seq = m // batch
    t = ATT_TILE
    nq = seq // t
    n_sub = SB_SUB_TILES
    steps = nq // n_sub
    n_pairs = N_SB_HEADS * HEAD_DIM // LANES
    return pl.pallas_call(
        _sb_kernel,
        grid=(batch, n_pairs, steps),
        in_specs=[
            pl.BlockSpec((None, n_sub, LANES, t), lambda b, p, i: (b, i, q_row + p, 0)),
            pl.BlockSpec((seq, LANES), lambda b, p, i: (b, k_col + p)),
            pl.BlockSpec((None, nq, LANES, t), lambda b, p, i: (b, 0, v_row + p, 0)),
            _resident(tri.shape, lambda b, p, i: (0, 0)),
        ],
        out_specs=pl.BlockSpec((n_sub * t, LANES), lambda b, p, i: (b * steps + i, p)),
        out_shape=jax.ShapeDtypeStruct((m, n_pairs * LANES), BF16),
        scratch_shapes=[pltpu.VMEM((2 * n_sub, LANES, t), F32)],
        compiler_params=_params("parallel", "parallel", "arbitrary"),
        name="sb_attn",
    )(cols, rows, cols, tri)


def _fox_kernel(qt_ref, k_ref, vt_ref, cend_ref, o_ref, y_ref, p_ref, acc_ref):
    t = ATT_TILE
    qi = pl.program_id(2)
    sub = lax.broadcasted_iota(jnp.int32, (LANES, t), 0)
    own = (sub < HEAD_DIM, sub >= HEAD_DIM)
    n_tiles = cend_ref.shape[0] - 1
    k_max = cend_ref[n_tiles:n_tiles + 1, :]

    n_sub = qt_ref.shape[0]
    chains = [(s, hh) for s in range(n_sub) for hh in range(2)]
    n_ch = len(chains)
    diag = [n_sub * qi + s for s in range(n_sub)]
    minus_c = jnp.where(sub < 3, -1.0, 0.0).astype(BF16)
    q_aug, z_bound = [], []
    for s, hh in chains:
        q_pair = qt_ref[s]
        qf = q_pair.astype(F32)
        z_bound.append(jnp.sqrt(jnp.sum(jnp.where(own[hh], qf * qf, 0.0), axis=0, keepdims=True))
                       * (k_max[:, hh:hh + 1] * NORM_SLACK))
        q_aug.append(jnp.concatenate([jnp.where(own[hh], q_pair, jnp.zeros_like(q_pair)), minus_c],
                                     axis=0))
    one = jnp.ones((LANES, t), BF16)

    def logits(c, j):
        kb = k_ref[pl.ds(pl.multiple_of(j * t, t), t), :]
        lhs = kb[:, :2 * LANES] if chains[c][1] == 0 else jnp.concatenate(
            [kb[:, :LANES], kb[:, 2 * LANES:]], axis=1)
        return jnp.dot(lhs, q_aug[c], preferred_element_type=F32)

    def softmax_step(y, m_prev, skip=None):
        m_new = jnp.maximum(m_prev, jnp.max(y, axis=0, keepdims=True))
        shift = m_new
        if skip is not None:
            m_new = jnp.where(skip, m_prev, m_new)
            shift = jnp.where(skip, -MASKED, m_new)
        return jnp.exp(y - shift).astype(BF16), jnp.exp(m_prev - m_new), m_new

    def weighted_values(c, j):
        vt = jnp.where(own[chains[c][1]], vt_ref[j], one)
        return jnp.dot(vt, p_ref[c], preferred_element_type=F32)

    def alive(r_next, ms):
        done = []
        for c, (s, hh) in enumerate(chains):
            j = diag[s] - r_next
            far = cend_ref[pl.ds(jnp.maximum(j, 0), 1), :][:, hh:hh + 1]
            dead = jnp.max(z_bound[c] - far - ms[c]) < -EXP_DEAD
            done.append(jnp.logical_or(j < 0, dead))
        all_done = done[0]
        for d in done[1:]:
            all_done = jnp.logical_and(all_done, d)
        return jnp.logical_not(all_done).astype(jnp.int32)

    key = lax.broadcasted_iota(jnp.int32, (t, t), 0)
    query = lax.broadcasted_iota(jnp.int32, (t, t), 1)
    causal = key <= query
    yd = [logits(c, diag[s]) for c, (s, hh) in enumerate(chains)]
    yn = [logits(c, jnp.maximum(diag[s] - 1, 0)) for c, (s, hh) in enumerate(chains)]
    ms, alphas = [], []
    for c in range(n_ch):
        p, _, m_new = softmax_step(jnp.where(causal, yd[c], MASKED), jnp.full((1, t), MASKED, F32))
        p_ref[c] = p
        y_ref[c] = yn[c]
        acc_ref[c] = jnp.zeros((LANES, t), F32)
        ms.append(m_new)
        alphas.append(jnp.ones((1, t), F32))

    def cond(state):
        r, live = state[0], state[1]
        return jnp.logical_and(r <= diag[n_sub - 1], live > 0)

    def body(state):
        r = state[0]
        ms, alphas = state[2:2 + n_ch], state[2 + n_ch:]
        live = alive(r + 1, ms)
        pv = [weighted_values(c, jnp.maximum(diag[s] - r + 1, 0)) for c, (s, hh) in enumerate(chains)]
        yn = [logits(c, jnp.maximum(diag[s] - r - 1, 0)) for c, (s, hh) in enumerate(chains)]
        new_m, new_a = [], []
        for c, (s, hh) in enumerate(chains):
            p, alpha, m_new = softmax_step(y_ref[c], ms[c],
                                           diag[s] - r < 0 if s < n_sub - 1 else None)
            acc_ref[c] = alphas[c] * acc_ref[c] + pv[c]
            p_ref[c] = p
            y_ref[c] = yn[c]
            new_m.append(m_new)
            new_a.append(alpha)
        return (r + 1, live) + tuple(new_m) + tuple(new_a)

    state = lax.while_loop(cond, body, (jnp.int32(1), alive(1, ms)) + tuple(ms) + tuple(alphas))
    r_end = state[0]
    for s in range(n_sub):
        outs = []
        for hh in range(2):
            c = 2 * s + hh
            acc = state[2 + n_ch + c] * acc_ref[c] + weighted_values(
                c, jnp.maximum(diag[s] - r_end + 1, 0))
            outs.append(acc / pltpu.roll(acc, HEAD_DIM, axis=0))
        o_ref[s * t:(s + 1) * t, :] = jnp.where(own[0], outs[0], outs[1]).T.astype(o_ref.dtype)


def _fox_attention(rows, cols, c_ends, batch):
    m = rows.shape[0]
    seq = m // batch
    t = ATT_TILE
    nq = seq // t
    n_pairs = N_FOX_HEADS * HEAD_DIM // LANES
    n_sub = FOX_SUB_TILES
    steps = nq // n_sub
    n_ch = 2 * n_sub
    return pl.pallas_call(
        _fox_kernel,
        grid=(batch, n_pairs, steps),
        in_specs=[
            pl.BlockSpec((None, n_sub, LANES, t), lambda b, p, i: (b, i, p, 0)),
            pl.BlockSpec((seq, 3 * LANES), lambda b, p, i: (b, p)),
            pl.BlockSpec((None, nq, LANES, t), lambda b, p, i: (b, 0, n_pairs + p, 0)),
            pl.BlockSpec((None, None, nq + 1, 2), lambda b, p, i: (b, p, 0, 0)),
        ],
        out_specs=pl.BlockSpec((n_sub * t, LANES), lambda b, p, i: (b * steps + i, p)),
        out_shape=jax.ShapeDtypeStruct((m, n_pairs * LANES), BF16),
        scratch_shapes=[
            pltpu.VMEM((n_ch, t, t), F32),
            pltpu.VMEM((n_ch, t, t), BF16),
            pltpu.VMEM((n_ch, LANES, t), F32),
        ],
        compiler_params=_params("parallel", "parallel", "arbitrary"),
        name="fox_attn",
    )(cols, rows, cols, c_ends)


def _memkv_kernel(mem_ref, g_ref, w_ref, o_ref):
    mem_n = _rms(mem_ref[...], g_ref[...]).astype(BF16)
    o_ref[...] = jnp.dot(mem_n, w_ref[...], preferred_element_type=F32).astype(BF16)


def _memkv(mem2d, g, w_mem_kv):
    n_layers, d, n_out = w_mem_kv.shape
    rows = mem2d.shape[0]
    return pl.pallas_call(
        _memkv_kernel,
        grid=(n_layers,),
        in_specs=[
            _resident((rows, d), lambda l: (0, 0)),
            _resident((1, d), lambda l: (0, 0)),
            pl.BlockSpec((None, d, n_out), lambda l: (l, 0, 0)),
        ],
        out_specs=pl.BlockSpec((None, rows, n_out), lambda l: (l, 0, 0)),
        out_shape=jax.ShapeDtypeStruct((n_layers, rows, n_out), BF16),
        compiler_params=_params("parallel"),
        name="memkv",
    )(mem2d, g, w_mem_kv)


def _post_kernel(l_ref, h_ref, qmem_ref, osb_ref, ofx_ref, kv_ref, gpre_ref, gpost_ref,
                 wgate_ref, bgate_ref, wsb_ref, wfx_ref, wmem_ref, wout_ref, o_ref):
    x = h_ref[...]
    d = x.shape[1]
    u = _rms(x, gpre_ref[...]).astype(BF16)
    gates = jax.nn.sigmoid(jnp.dot(u, wgate_ref[...], preferred_element_type=F32) + bgate_ref[...])

    qm = qmem_ref[...]
    kv = kv_ref[...]
    mem_w = N_MEM_HEADS * MEM_HEAD_DIM
    scale = MEM_HEAD_DIM ** -0.5
    heads = []
    for hh in range(N_MEM_HEADS):
        lo, hi = hh * MEM_HEAD_DIM, (hh + 1) * MEM_HEAD_DIM
        z = lax.dot_general(qm[:, lo:hi], kv[:, lo:hi], _NT, preferred_element_type=F32) * scale
        e = jnp.exp(z - jnp.max(z, axis=1, keepdims=True))
        p = (e / jnp.sum(e, axis=1, keepdims=True)).astype(BF16)
        heads.append(jnp.dot(p, kv[:, mem_w + lo:mem_w + hi], preferred_element_type=F32))
    o_mem = jnp.concatenate(heads, axis=1).astype(BF16)

    merged = (gates[:, 0:d] * jnp.dot(osb_ref[...], wsb_ref[...], preferred_element_type=F32)
              + gates[:, d:2 * d] * jnp.dot(ofx_ref[...], wfx_ref[...], preferred_element_type=F32)
              + gates[:, 2 * d:3 * d] * jnp.dot(o_mem, wmem_ref[...], preferred_element_type=F32))
    y = jnp.dot(merged.astype(BF16), wout_ref[...], preferred_element_type=F32)
    o_ref[...] = x + _rms(y, gpost_ref[...])


def _post(l, h, qkv, o_sb, o_fx, kvmem, g_pre, g_post, w_gate, b_gate, w_sb, w_fx, w_mem, w_out,
          batch, qmem_col):
    m, d = h.shape
    seq = m // batch
    tm = WIDE_ROW_TILE
    nt = seq // tm
    mem_len = kvmem.shape[1] // batch
    mem_w = N_MEM_HEADS * MEM_HEAD_DIM
    wmap = lambda b, i, l: (l[0], 0, 0)
    rows = lambda b, i, l: (b * nt + i, 0)
    return pl.pallas_call(
        _post_kernel,
        grid_spec=pltpu.PrefetchScalarGridSpec(
            num_scalar_prefetch=1,
            grid=(batch, nt),
            in_specs=[
                pl.BlockSpec((tm, d), rows),
                pl.BlockSpec((tm, mem_w), lambda b, i, l: (b * nt + i, qmem_col)),
                pl.BlockSpec((tm, o_sb.shape[1]), rows),
                pl.BlockSpec((tm, o_fx.shape[1]), rows),
                pl.BlockSpec((None, mem_len, 2 * mem_w), lambda b, i, l: (l[0], b, 0)),
                _resident((None, 1, d), wmap),
                _resident((None, 1, d), wmap),
                _resident((None, d, N_BRANCH * d), wmap),
                _resident((None, 1, N_BRANCH * d), wmap),
                _resident((None,) + w_sb.shape[1:], wmap),
                _resident((None,) + w_fx.shape[1:], wmap),
                _resident((None,) + w_mem.shape[1:], wmap),
                _resident((None, d, d), wmap),
            ],
            out_specs=pl.BlockSpec((tm, d), rows),
        ),
        out_shape=jax.ShapeDtypeStruct((m, d), F32),
        compiler_params=_params("parallel", "parallel"),
        name="post",
    )(l, h, qkv, o_sb, o_fx, kvmem, g_pre, g_post, w_gate, b_gate, w_sb, w_fx, w_mem, w_out)


def kernel(x, mem, ffn1_pre_g, ffn1_post_g, ffn1_w_gate, ffn1_w_up, ffn1_w_down, mix_pre_g, mix_post_g, w_in, b_forget, mem_norm_g, w_mem_kv, w_gate, b_gate, w_br_sb, w_br_fox, w_br_mem, w_out, ffn2_pre_g, ffn2_post_g, ffn2_w_gate, ffn2_w_up, ffn2_w_down):
    batch, seq, d = x.shape
    n_layers = w_in.shape[0]
    sb_w = N_SB_HEADS * HEAD_DIM
    fox_w = N_FOX_HEADS * HEAD_DIM
    mem_w = N_MEM_HEADS * MEM_HEAD_DIM
    assert seq % WIDE_ROW_TILE == 0 and WIDE_ROW_TILE % ROW_TILE == 0 and d % LANES == 0
    assert seq % (max(FOX_SUB_TILES, SB_SUB_TILES) * ATT_TILE) == 0 and ROW_TILE % ATT_TILE == 0
    assert w_in.shape[2] == 3 * sb_w + 3 * fox_w + N_FOX_HEADS + mem_w

    scale = HEAD_DIM ** -0.5
    fx = 3 * sb_w
    f_lo = fx + 3 * fox_w
    f_hi = f_lo + N_FOX_HEADS
    bf = lambda w: w.astype(BF16)
    vec = lambda g: g[:, None, :]

    order = jnp.argsort(b_forget, axis=1)

    def by_head(w, axis):
        shape = [1] * w.ndim
        shape[0], shape[axis] = order.shape[0], order.shape[1]
        return jnp.take_along_axis(w, order.reshape(shape), axis=axis)

    def fox_cols(lo):
        w = w_in[:, :, lo:lo + fox_w].reshape(n_layers, d, N_FOX_HEADS, HEAD_DIM)
        return by_head(w, 2).reshape(n_layers, d, fox_w)

    w_gate_f = by_head(w_in[:, :, f_lo:f_hi], 2)
    g_hi = w_gate_f.astype(BF16)
    g_mid = (w_gate_f - g_hi.astype(F32)).astype(BF16)
    g_lo = ((w_gate_f - g_hi.astype(F32)) - g_mid.astype(F32)).astype(BF16)
    gate3 = jnp.pad(jnp.concatenate([g_hi, g_mid, g_lo], axis=2),
                    ((0, 0), (0, 0), (0, LANES - 3 * N_FOX_HEADS)))
    w_main = jnp.concatenate([bf(fox_cols(fx + fox_w)), bf(w_in[:, :, sb_w:2 * sb_w]),
                              bf(w_in[:, :, f_hi:])], axis=2)
    w_t = jnp.swapaxes(jnp.concatenate(
        [fox_cols(fx) * scale, fox_cols(fx + 2 * fox_w),
         w_in[:, :, :sb_w] * (scale * LOG2_E), w_in[:, :, 2 * sb_w:fx]], axis=2), 1, 2).astype(BF16)
    b_f = jnp.pad(by_head(b_forget, 1), ((0, 0), (0, LANES - N_FOX_HEADS)))[:, None, :]
    w_br_fox_sorted = by_head(w_br_fox.reshape(n_layers, N_FOX_HEADS, HEAD_DIM, d), 1).reshape(
        w_br_fox.shape)
    term, head = jnp.meshgrid(jnp.arange(3), jnp.arange(N_FOX_HEADS), indexing="ij")
    place = jnp.zeros((LANES, N_FOX_HEADS * LANES), BF16).at[
        term * N_FOX_HEADS + head, head * LANES + term].set(1.0)
    tok = jnp.arange(ROW_TILE)
    lower = (tok[:, None] >= tok[None, :]).astype(BF16)
    weights = dict(
        f1=(vec(ffn1_pre_g), vec(ffn1_post_g), bf(ffn1_w_gate), bf(ffn1_w_up), bf(ffn1_w_down)),
        f2=(vec(ffn2_pre_g), vec(ffn2_post_g), bf(ffn2_w_gate), bf(ffn2_w_up), bf(ffn2_w_down)),
        proj=(vec(mix_pre_g), w_main, w_t, gate3, b_f, lower, place),
        post=(vec(mix_pre_g), vec(mix_post_g), bf(w_gate), vec(b_gate), bf(w_br_sb),
              bf(w_br_fox_sorted), bf(w_br_mem), bf(w_out)),
    )

    t = ATT_TILE
    tri = (jnp.arange(t)[None, :] > jnp.arange(t)[:, None]).astype(BF16)

    kvmem = _memkv(mem.reshape(batch * mem.shape[1], d), mem_norm_g[None, :], bf(w_mem_kv))

    fox_k_w = (fox_w // LANES) * 3 * LANES
    k_sb_col = fox_k_w // LANES
    qmem_col = (fox_k_w + sb_w) // mem_w
    q_sb_row = 2 * fox_w // LANES
    v_sb_row = q_sb_row + sb_w // LANES
    nq = seq // t

    h = x.reshape(batch * seq, d)
    for layer in range(n_layers):
        l = jnp.full((1,), layer, jnp.int32)
        h = _ffn(l, h, *weights["f1"])
        rows, cols, c = _proj(l, h, *weights["proj"], batch)
        o_sb = _sb_attention(rows, cols, tri, batch, k_sb_col, q_sb_row, v_sb_row)
        c = c.reshape(batch, nq, t, LANES)
        c_ends = c[:, :, t - 1, :N_FOX_HEADS]
        k_max = jnp.sqrt(jnp.max(c[:, :, :, KNORM_LANE:KNORM_LANE + N_FOX_HEADS], axis=(1, 2)))
        c_ends = jnp.concatenate([c_ends, k_max[:, None, :]], axis=1)
        c_ends = c_ends.reshape(batch, nq + 1, N_FOX_HEADS // 2, 2).transpose(0, 2, 1, 3)
        o_fx = _fox_attention(rows, cols, c_ends, batch)
        h = _post(l, h, rows, o_sb, o_fx, kvmem, *weights["post"], batch, qmem_col)
        h = _ffn(l, h, *weights["f2"])
    return h.reshape(batch, seq, d)
```

```python
import math

import jax
import jax.numpy as jnp
from jax import lax
from jax.experimental import pallas as pl
from jax.experimental.pallas import tpu as pltpu

F32 = jnp.float32
BF16 = jnp.bfloat16

HEAD_DIM = 64
N_SB_HEADS = 8
N_FOX_HEADS = 8
N_MEM_HEADS = 4
MEM_HEAD_DIM = 128
N_BRANCH = 3
RMS_EPS = 1e-6
LOG2_E = math.log2(math.e)

LANES = 128
MXU_WIDTH = 256
VMEM_LIMIT_BYTES = 56 * 1024 * 1024
ROW_TILE = 512
WIDE_ROW_TILE = 1024
FFN_CHUNKS = 2
ATT_TILE = 256
FOX_SUB_TILES = 4
SB_SUB_TILES = 8
KNORM_LANE = 32
EXP_DEAD = 93.0
EXP2_DEAD = 135.0
SOFTPLUS2_CLAMP = 86.0
NORM_SLACK = 1.01
MASKED = -1e30

_NT = (((1,), (1,)), ((), ()))


def _rms(x, g):
    return x * lax.rsqrt(jnp.mean(x * x, axis=-1, keepdims=True) + RMS_EPS) * g


def _params(*sem):
    return pltpu.CompilerParams(dimension_semantics=sem, vmem_limit_bytes=VMEM_LIMIT_BYTES)


def _resident(shape, index_map):
    return pl.BlockSpec(shape, index_map, pipeline_mode=pl.Buffered(1))


def _ffn_kernel(l_ref, h_ref, gpre_ref, gpost_ref, wg_ref, wu_ref, wd_ref, o_ref):
    x = h_ref[...]
    u = _rms(x, gpre_ref[...]).astype(BF16)
    n_mxu = wg_ref.shape[1] // MXU_WIDTH
    cuts = [MXU_WIDTH * ((n_mxu * k + FFN_CHUNKS - 1) // FFN_CHUNKS) for k in range(FFN_CHUNKS + 1)]
    f = None
    for k in range(FFN_CHUNKS):
        cols = slice(cuts[k], cuts[k + 1])
        g = jnp.dot(u, wg_ref[:, cols], preferred_element_type=F32)
        up = jnp.dot(u, wu_ref[:, cols], preferred_element_type=F32)
        a = (g * jax.nn.sigmoid(g) * up).astype(BF16)
        part = jnp.dot(a, wd_ref[cols, :], preferred_element_type=F32)
        f = part if f is None else f + part
    o_ref[...] = x + 0.5 * _rms(f, gpost_ref[...])


def _ffn(l, h, g_pre, g_post, w_gate, w_up, w_down):
    m, d = h.shape
    f = w_gate.shape[-1]
    tm = WIDE_ROW_TILE
    wmap = lambda i, l: (l[0], 0, 0)
    return pl.pallas_call(
        _ffn_kernel,
        grid_spec=pltpu.PrefetchScalarGridSpec(
            num_scalar_prefetch=1,
            grid=(m // tm,),
            in_specs=[
                pl.BlockSpec((tm, d), lambda i, l: (i, 0)),
                _resident((None, 1, d), wmap),
                _resident((None, 1, d), wmap),
                _resident((None, d, f), wmap),
                _resident((None, d, f), wmap),
                _resident((None, f, d), wmap),
            ],
            out_specs=pl.BlockSpec((tm, d), lambda i, l: (i, 0)),
        ),
        out_shape=jax.ShapeDtypeStruct((m, d), F32),
        compiler_params=_params("parallel"),
        name="ffn",
    )(l, h, g_pre, g_post, w_gate, w_up, w_down)


def _split3(x):
    hi = x.astype(BF16).astype(F32)
    mid = (x - hi).astype(BF16).astype(F32)
    lo = ((x - hi) - mid).astype(BF16).astype(F32)
    return hi, mid, lo


def _fold3(x, nh):
    n = x.shape[1]
    return x + pltpu.roll(x, n - nh, axis=1) + pltpu.roll(x, n - 2 * nh, axis=1)


def _spread3(terms, nh):
    lane = lax.broadcasted_iota(jnp.int32, terms[0].shape, 1)
    out = jnp.where(lane < nh, terms[0], 0.0)
    for i in (1, 2):
        moved = pltpu.roll(terms[i], i * nh, axis=1)
        out = jnp.where(jnp.logical_and(lane >= i * nh, lane < (i + 1) * nh), moved, out)
    return out.astype(BF16)


def _proj_kernel(l_ref, h_ref, g_ref, w_ref, wt_ref, wg_ref, bf_ref, lower_ref, place_ref,
                 rows_ref, cols_ref, c_ref, carry_ref):
    @pl.when(pl.program_id(1) == 0)
    def _():
        carry_ref[...] = jnp.zeros_like(carry_ref)

    nh = N_FOX_HEADS
    u = _rms(h_ref[...], g_ref[...])
    ub = u.astype(BF16)
    u_mid = (u - ub.astype(F32)).astype(BF16)

    w_gate3 = wg_ref[...]
    e_hi = jnp.dot(ub, w_gate3, preferred_element_type=F32)
    e_mid = jnp.dot(u_mid, w_gate3, preferred_element_type=F32)
    drop_lo = lax.broadcasted_iota(jnp.int32, e_mid.shape, 1) < 2 * nh
    fl = _fold3(e_hi + jnp.where(drop_lo, e_mid, 0.0), nh) + bf_ref[...]
    log_f = jnp.minimum(fl, 0.0) - jnp.log1p(jnp.exp(-jnp.abs(fl)))

    sums = jnp.dot(lower_ref[...], _spread3(_split3(log_f), nh), preferred_element_type=F32)
    c = _fold3(sums, nh) + carry_ref[0:1, :]
    tm = c.shape[0]
    carry_ref[...] = jnp.broadcast_to(c[tm - 1:tm, :], carry_ref.shape)
    placed = jnp.dot(_spread3(_split3(c), nh), place_ref[...], preferred_element_type=F32)

    main = jnp.dot(ub, w_ref[...], preferred_element_type=F32)
    n_k = N_FOX_HEADS * HEAD_DIM
    kf = main[:, :n_k]
    dim = lax.broadcasted_iota(jnp.int32, (n_k, LANES), 0)
    lane_k = lax.broadcasted_iota(jnp.int32, (n_k, LANES), 1)
    head_of_dim = jnp.where(lane_k == KNORM_LANE + dim // HEAD_DIM, 1.0, 0.0).astype(BF16)
    norms2 = jnp.dot((kf * kf).astype(BF16), head_of_dim, preferred_element_type=F32)
    lane_c = lax.broadcasted_iota(jnp.int32, c.shape, 1)
    c_ref[...] = jnp.where(jnp.logical_and(lane_c >= KNORM_LANE, lane_c < KNORM_LANE + nh),
                           norms2, c)
    pair_w = 3 * LANES
    for p in range(n_k // LANES):
        base = p * pair_w
        rows_ref[:, base:base + LANES] = main[:, p * LANES:(p + 1) * LANES].astype(BF16)
        rows_ref[:, base + LANES:base + pair_w] = (
            placed[:, 2 * p * LANES:2 * (p + 1) * LANES].astype(BF16))
    rows_ref[:, (n_k // LANES) * pair_w:] = main[:, n_k:].astype(BF16)

    across = lax.dot_general(wt_ref[...], ub, _NT, preferred_element_type=F32).astype(BF16)
    t = ATT_TILE
    for s in range(cols_ref.shape[0]):
        cols_ref[s] = across[:, s * t:(s + 1) * t]


def _proj(l, h, g, w_main, w_t, w_gate3, b_f, lower, place, batch):
    m, d = h.shape
    seq = m // batch
    tm = ROW_TILE
    nt = seq // tm
    per = tm // ATT_TILE
    n_k = N_FOX_HEADS * HEAD_DIM
    n_rows_out = w_main.shape[-1] - n_k + (n_k // LANES) * 3 * LANES
    n_cols_out = w_t.shape[1]
    wmap = lambda b, i, l: (l[0], 0, 0)
    const = lambda b, i, l: (0, 0)
    return pl.pallas_call(
        _proj_kernel,
        grid_spec=pltpu.PrefetchScalarGridSpec(
            num_scalar_prefetch=1,
            grid=(batch, nt),
            in_specs=[
                pl.BlockSpec((tm, d), lambda b, i, l: (b * nt + i, 0)),
                _resident((None, 1, d), wmap),
                _resident((None, d, w_main.shape[-1]), wmap),
                _resident((None, n_cols_out, d), wmap),
                _resident((None, d, LANES), wmap),
                _resident((None, 1, LANES), wmap),
                _resident(lower.shape, const),
                _resident(place.shape, const),
            ],
            out_specs=[
                pl.BlockSpec((tm, n_rows_out), lambda b, i, l: (b * nt + i, 0)),
                pl.BlockSpec((None, per, n_cols_out, ATT_TILE), lambda b, i, l: (b, i, 0, 0)),
                pl.BlockSpec((tm, LANES), lambda b, i, l: (b * nt + i, 0)),
            ],
            scratch_shapes=[pltpu.VMEM((8, LANES), F32)],
        ),
        out_shape=[
            jax.ShapeDtypeStruct((m, n_rows_out), BF16),
            jax.ShapeDtypeStruct((batch, seq // ATT_TILE, n_cols_out, ATT_TILE), BF16),
            jax.ShapeDtypeStruct((m, LANES), F32),
        ],
        compiler_params=_params("parallel", "arbitrary"),
        name="proj",
    )(l, h, g, w_main, w_t, w_gate3, b_f, lower, place)


def _softplus2(z2):
    return jnp.maximum(z2, jnp.log2(1.0 + jnp.exp2(jnp.minimum(z2, SOFTPLUS2_CLAMP))))


def _sb_kernel(qt_ref, k_ref, vt_ref, tri_ref, o_ref, acc_ref):
    t = ATT_TILE
    qi = pl.program_id(2)
    sub = lax.broadcasted_iota(jnp.int32, (LANES, t), 0)
    own = (sub < HEAD_DIM, sub >= HEAD_DIM)
    n_sub = qt_ref.shape[0]
    chains = [(s, hh) for s in range(n_sub) for hh in range(2)]
    n_ch = len(chains)
    diag = [n_sub * qi + s for s in range(n_sub)]
    q_heads = []
    for s, hh in chains:
        q_pair = qt_ref[s]
        q_heads.append(jnp.where(own[hh], q_pair, jnp.zeros_like(q_pair)))
    tri = tri_ref[...]

    key = lax.broadcasted_iota(jnp.int32, (t, t), 0)
    query = lax.broadcasted_iota(jnp.int32, (t, t), 1)
    strictly_causal = key < query

    def scores(c, j, mask):
        kb = k_ref[pl.ds(pl.multiple_of(j * t, t), t), :]
        z = jnp.dot(kb, q_heads[c], preferred_element_type=F32)
        return z if mask is None else jnp.where(mask, z, MASKED)

    def decay(z):
        sp = _softplus2(z)
        later = jnp.dot(tri, sp.astype(BF16), preferred_element_type=F32)
        return sp, later, jnp.sum(sp, axis=0, keepdims=True)

    def weigh(z, sp, later, carry, j):
        w = jnp.exp2((z - sp) - later - carry)
        return jnp.dot(vt_ref[j], w.astype(BF16), preferred_element_type=F32)

    prev = [jnp.maximum(diag[s] - 1, 0) for s, hh in chains]
    zd = [scores(c, diag[s], strictly_causal) for c, (s, hh) in enumerate(chains)]
    zp = [scores(c, prev[c], None) for c in range(n_ch)]
    dd = [decay(z) for z in zd]
    dp = [decay(z) for z in zp]
    pvd = [weigh(zd[c], dd[c][0], dd[c][1], 0.0, diag[s]) for c, (s, hh) in enumerate(chains)]
    pvp = [weigh(zp[c], dp[c][0], dp[c][1], dd[c][2], prev[c]) for c in range(n_ch)]
    carries = []
    for c, (s, hh) in enumerate(chains):
        acc_ref[c] = pvd[c] + jnp.where(diag[s] > 0, pvp[c], 0.0)
        carries.append(dd[c][2] + dp[c][2])

    def alive(r_next, cs):
        more = [jnp.logical_and(diag[s] - r_next >= 0, jnp.min(cs[c]) < EXP2_DEAD)
                for c, (s, hh) in enumerate(chains)]
        any_more = more[0]
        for m in more[1:]:
            any_more = jnp.logical_or(any_more, m)
        return any_more.astype(jnp.int32)

    def cond(state):
        return state[1] > 0

    def body(state):
        r, cs = state[0], state[2:]
        js = [diag[s] - r for s, hh in chains]
        zs = [scores(c, jnp.maximum(js[c], 0), None) for c in range(n_ch)]
        ds = [decay(z) for z in zs]
        new = []
        for c in range(n_ch):
            pv = weigh(zs[c], ds[c][0], ds[c][1], cs[c], jnp.maximum(js[c], 0))
            acc_ref[c] += jnp.where(js[c] >= 0, pv, 0.0)
            new.append(cs[c] + ds[c][2])
        return (r + 1, alive(r + 1, new)) + tuple(new)

    lax.while_loop(cond, body, (jnp.int32(2), alive(2, carries)) + tuple(carries))
    for s in range(n_sub):
        o_ref[s * t:(s + 1) * t, :] = jnp.where(
            own[0], acc_ref[2 * s], acc_ref[2 * s + 1]).T.astype(o_ref.dtype)


def _sb_attention(rows, cols, tri, batch, k_col, q_row, v_row):
    m = rows.shape[0]
    seq = m // batch
    t = ATT_TILE
    nq = seq // t
    n_sub = SB_SUB_TILES
    steps = nq // n_sub
    n_pairs = N_SB_HEADS * HEAD_DIM // LANES
    return pl.pallas_call(
        _sb_kernel,
        grid=(batch, n_pairs, steps),
        in_specs=[
            pl.BlockSpec((None, n_sub, LANES, t), lambda b, p, i: (b, i, q_row + p, 0)),
            pl.BlockSpec((seq, LANES), lambda b, p, i: (b, k_col + p)),
            pl.BlockSpec((None, nq, LANES, t), lambda b, p, i: (b, 0, v_row + p, 0)),
            _resident(tri.shape, lambda b, p, i: (0, 0)),
        ],
        out_specs=pl.BlockSpec((n_sub * t, LANES), lambda b, p, i: (b * steps + i, p)),
        out_shape=jax.ShapeDtypeStruct((m, n_pairs * LANES), BF16),
        scratch_shapes=[pltpu.VMEM((2 * n_sub, LANES, t), F32)],
        compiler_params=_params("parallel", "parallel", "arbitrary"),
        name="sb_attn",
    )(cols, rows, cols, tri)


def _fox_kernel(qt_ref, k_ref, vt_ref, cend_ref, o_ref, y_ref, p_ref, acc_ref):
    t = ATT_TILE
    qi = pl.program_id(2)
    sub = lax.broadcasted_iota(jnp.int32, (LANES, t), 0)
    own = (sub < HEAD_DIM, sub >= HEAD_DIM)
    n_tiles = cend_ref.shape[0] - 1
    k_max = cend_ref[n_tiles:n_tiles + 1, :]

    n_sub = qt_ref.shape[0]
    chains = [(s, hh) for s in range(n_sub) for hh in range(2)]
    n_ch = len(chains)
    diag = [n_sub * qi + s for s in range(n_sub)]
    minus_c = jnp.where(sub < 3, -1.0, 0.0).astype(BF16)
    q_aug, z_bound = [], []
    for s, hh in chains:
        q_pair = qt_ref[s]
        qf = q_pair.astype(F32)
        z_bound.append(jnp.sqrt(jnp.sum(jnp.where(own[hh], qf * qf, 0.0), axis=0, keepdims=True))
                       * (k_max[:, hh:hh + 1] * NORM_SLACK))
        q_aug.append(jnp.concatenate([jnp.where(own[hh], q_pair, jnp.zeros_like(q_pair)), minus_c],
                                     axis=0))
    one = jnp.ones((LANES, t), BF16)

    def logits(c, j):
        kb = k_ref[pl.ds(pl.multiple_of(j * t, t), t), :]
        lhs = kb[:, :2 * LANES] if chains[c][1] == 0 else jnp.concatenate(
            [kb[:, :LANES], kb[:, 2 * LANES:]], axis=1)
        return jnp.dot(lhs, q_aug[c], preferred_element_type=F32)

    def softmax_step(y, m_prev, skip=None):
        m_new = jnp.maximum(m_prev, jnp.max(y, axis=0, keepdims=True))
        shift = m_new
        if skip is not None:
            m_new = jnp.where(skip, m_prev, m_new)
            shift = jnp.where(skip, -MASKED, m_new)
        return jnp.exp(y - shift).astype(BF16), jnp.exp(m_prev - m_new), m_new

    def weighted_values(c, j):
        vt = jnp.where(own[chains[c][1]], vt_ref[j], one)
        return jnp.dot(vt, p_ref[c], preferred_element_type=F32)

    def alive(r_next, ms):
        done = []
        for c, (s, hh) in enumerate(chains):
            j = diag[s] - r_next
            far = cend_ref[pl.ds(jnp.maximum(j, 0), 1), :][:, hh:hh + 1]
            dead = jnp.max(z_bound[c] - far - ms[c]) < -EXP_DEAD
            done.append(jnp.logical_or(j < 0, dead))
        all_done = done[0]
        for d in done[1:]:
            all_done = jnp.logical_and(all_done, d)
        return jnp.logical_not(all_done).astype(jnp.int32)

    key = lax.broadcasted_iota(jnp.int32, (t, t), 0)
    query = lax.broadcasted_iota(jnp.int32, (t, t), 1)
    causal = key <= query
    yd = [logits(c, diag[s]) for c, (s, hh) in enumerate(chains)]
    yn = [logits(c, jnp.maximum(diag[s] - 1, 0)) for c, (s, hh) in enumerate(chains)]
    ms, alphas = [], []
    for c in range(n_ch):
        p, _, m_new = softmax_step(jnp.where(causal, yd[c], MASKED), jnp.full((1, t), MASKED, F32))
        p_ref[c] = p
        y_ref[c] = yn[c]
        acc_ref[c] = jnp.zeros((LANES, t), F32)
        ms.append(m_new)
        alphas.append(jnp.ones((1, t), F32))

    def cond(state):
        r, live = state[0], state[1]
        return jnp.logical_and(r <= diag[n_sub - 1], live > 0)

    def body(state):
        r = state[0]
        ms, alphas = state[2:2 + n_ch], state[2 + n_ch:]
        live = alive(r + 1, ms)
        pv = [weighted_values(c, jnp.maximum(diag[s] - r + 1, 0)) for c, (s, hh) in enumerate(chains)]
        yn = [logits(c, jnp.maximum(diag[s] - r - 1, 0)) for c, (s, hh) in enumerate(chains)]
        new_m, new_a = [], []
        for c, (s, hh) in enumerate(chains):
            p, alpha, m_new = softmax_step(y_ref[c], ms[c],
                                           diag[s] - r < 0 if s < n_sub - 1 else None)
            acc_ref[c] = alphas[c] * acc_ref[c] + pv[c]
            p_ref[c] = p
            y_ref[c] = yn[c]
            new_m.append(m_new)
            new_a.append(alpha)
        return (r + 1, live) + tuple(new_m) + tuple(new_a)

    state = lax.while_loop(cond, body, (jnp.int32(1), alive(1, ms)) + tuple(ms) + tuple(alphas))
    r_end = state[0]
    for s in range(n_sub):
        outs = []
        for hh in range(2):
            c = 2 * s + hh
            acc = state[2 + n_ch + c] * acc_ref[c] + weighted_values(
                c, jnp.maximum(diag[s] - r_end + 1, 0))
            outs.append(acc / pltpu.roll(acc, HEAD_DIM, axis=0))
        o_ref[s * t:(s + 1) * t, :] = jnp.where(own[0], outs[0], outs[1]).T.astype(o_ref.dtype)


def _fox_attention(rows, cols, c_ends, batch):
    m = rows.shape[0]
    seq = m // batch
    t = ATT_TILE
    nq = seq // t
    n_pairs = N_FOX_HEADS * HEAD_DIM // LANES
    n_sub = FOX_SUB_TILES
    steps = nq // n_sub
    n_ch = 2 * n_sub
    return pl.pallas_call(
        _fox_kernel,
        grid=(batch, n_pairs, steps),
        in_specs=[
            pl.BlockSpec((None, n_sub, LANES, t), lambda b, p, i: (b, i, p, 0)),
            pl.BlockSpec((seq, 3 * LANES), lambda b, p, i: (b, p)),
            pl.BlockSpec((None, nq, LANES, t), lambda b, p, i: (b, 0, n_pairs + p, 0)),
            pl.BlockSpec((None, None, nq + 1, 2), lambda b, p, i: (b, p, 0, 0)),
        ],
        out_specs=pl.BlockSpec((n_sub * t, LANES), lambda b, p, i: (b * steps + i, p)),
        out_shape=jax.ShapeDtypeStruct((m, n_pairs * LANES), BF16),
        scratch_shapes=[
            pltpu.VMEM((n_ch, t, t), F32),
            pltpu.VMEM((n_ch, t, t), BF16),
            pltpu.VMEM((n_ch, LANES, t), F32),
        ],
        compiler_params=_params("parallel", "parallel", "arbitrary"),
        name="fox_attn",
    )(cols, rows, cols, c_ends)


def _memkv_kernel(mem_ref, g_ref, w_ref, o_ref):
    mem_n = _rms(mem_ref[...], g_ref[...]).astype(BF16)
    o_ref[...] = jnp.dot(mem_n, w_ref[...], preferred_element_type=F32).astype(BF16)


def _memkv(mem2d, g, w_mem_kv):
    n_layers, d, n_out = w_mem_kv.shape
    rows = mem2d.shape[0]
    return pl.pallas_call(
        _memkv_kernel,
        grid=(n_layers,),
        in_specs=[
            _resident((rows, d), lambda l: (0, 0)),
            _resident((1, d), lambda l: (0, 0)),
            pl.BlockSpec((None, d, n_out), lambda l: (l, 0, 0)),
        ],
        out_specs=pl.BlockSpec((None, rows, n_out), lambda l: (l, 0, 0)),
        out_shape=jax.ShapeDtypeStruct((n_layers, rows, n_out), BF16),
        compiler_params=_params("parallel"),
        name="memkv",
    )(mem2d, g, w_mem_kv)


def _post_kernel(l_ref, h_ref, qmem_ref, osb_ref, ofx_ref, kv_ref, gpre_ref, gpost_ref,
                 wgate_ref, bgate_ref, wsb_ref, wfx_ref, wmem_ref, wout_ref, o_ref):
    x = h_ref[...]
    d = x.shape[1]
    u = _rms(x, gpre_ref[...]).astype(BF16)
    gates = jax.nn.sigmoid(jnp.dot(u, wgate_ref[...], preferred_element_type=F32) + bgate_ref[...])

    qm = qmem_ref[...]
    kv = kv_ref[...]
    mem_w = N_MEM_HEADS * MEM_HEAD_DIM
    scale = MEM_HEAD_DIM ** -0.5
    heads = []
    for hh in range(N_MEM_HEADS):
        lo, hi = hh * MEM_HEAD_DIM, (hh + 1) * MEM_HEAD_DIM
        z = lax.dot_general(qm[:, lo:hi], kv[:, lo:hi], _NT, preferred_element_type=F32) * scale
        e = jnp.exp(z - jnp.max(z, axis=1, keepdims=True))
        p = (e / jnp.sum(e, axis=1, keepdims=True)).astype(BF16)
        heads.append(jnp.dot(p, kv[:, mem_w + lo:mem_w + hi], preferred_element_type=F32))
    o_mem = jnp.concatenate(heads, axis=1).astype(BF16)

    merged = (gates[:, 0:d] * jnp.dot(osb_ref[...], wsb_ref[...], preferred_element_type=F32)
              + gates[:, d:2 * d] * jnp.dot(ofx_ref[...], wfx_ref[...], preferred_element_type=F32)
              + gates[:, 2 * d:3 * d] * jnp.dot(o_mem, wmem_ref[...], preferred_element_type=F32))
    y = jnp.dot(merged.astype(BF16), wout_ref[...], preferred_element_type=F32)
    o_ref[...] = x + _rms(y, gpost_ref[...])


def _post(l, h, qkv, o_sb, o_fx, kvmem, g_pre, g_post, w_gate, b_gate, w_sb, w_fx, w_mem, w_out,
          batch, qmem_col):
    m, d = h.shape
    seq = m // batch
    tm = WIDE_ROW_TILE
    nt = seq // tm
    mem_len = kvmem.shape[1] // batch
    mem_w = N_MEM_HEADS * MEM_HEAD_DIM
    wmap = lambda b, i, l: (l[0], 0, 0)
    rows = lambda b, i, l: (b * nt + i, 0)
    return pl.pallas_call(
        _post_kernel,
        grid_spec=pltpu.PrefetchScalarGridSpec(
            num_scalar_prefetch=1,
            grid=(batch, nt),
            in_specs=[
                pl.BlockSpec((tm, d), rows),
                pl.BlockSpec((tm, mem_w), lambda b, i, l: (b * nt + i, qmem_col)),
                pl.BlockSpec((tm, o_sb.shape[1]), rows),
                pl.BlockSpec((tm, o_fx.shape[1]), rows),
                pl.BlockSpec((None, mem_len, 2 * mem_w), lambda b, i, l: (l[0], b, 0)),
                _resident((None, 1, d), wmap),
                _resident((None, 1, d), wmap),
                _resident((None, d, N_BRANCH * d), wmap),
                _resident((None, 1, N_BRANCH * d), wmap),
                _resident((None,) + w_sb.shape[1:], wmap),
                _resident((None,) + w_fx.shape[1:], wmap),
                _resident((None,) + w_mem.shape[1:], wmap),
                _resident((None, d, d), wmap),
            ],
            out_specs=pl.BlockSpec((tm, d), rows),
        ),
        out_shape=jax.ShapeDtypeStruct((m, d), F32),
        compiler_params=_params("parallel", "parallel"),
        name="post",
    )(l, h, qkv, o_sb, o_fx, kvmem, g_pre, g_post, w_gate, b_gate, w_sb, w_fx, w_mem, w_out)


def kernel(x, mem, ffn1_pre_g, ffn1_post_g, ffn1_w_gate, ffn1_w_up, ffn1_w_down, mix_pre_g, mix_post_g, w_in, b_forget, mem_norm_g, w_mem_kv, w_gate, b_gate, w_br_sb, w_br_fox, w_br_mem, w_out, ffn2_pre_g, ffn2_post_g, ffn2_w_gate, ffn2_w_up, ffn2_w_down):
    batch, seq, d = x.shape
    n_layers = w_in.shape[0]
    sb_w = N_SB_HEADS * HEAD_DIM
    fox_w = N_FOX_HEADS * HEAD_DIM
    mem_w = N_MEM_HEADS * MEM_HEAD_DIM
    assert seq % WIDE_ROW_TILE == 0 and WIDE_ROW_TILE % ROW_TILE == 0 and d % LANES == 0
    assert seq % (max(FOX_SUB_TILES, SB_SUB_TILES) * ATT_TILE) == 0 and ROW_TILE % ATT_TILE == 0
    assert w_in.shape[2] == 3 * sb_w + 3 * fox_w + N_FOX_HEADS + mem_w

    scale = HEAD_DIM ** -0.5
    fx = 3 * sb_w
    f_lo = fx + 3 * fox_w
    f_hi = f_lo + N_FOX_HEADS
    bf = lambda w: w.astype(BF16)
    vec = lambda g: g[:, None, :]

    order = jnp.argsort(b_forget, axis=1)

    def by_head(w, axis):
        shape = [1] * w.ndim
        shape[0], shape[axis] = order.shape[0], order.shape[1]
        return jnp.take_along_axis(w, order.reshape(shape), axis=axis)

    def fox_cols(lo):
        w = w_in[:, :, lo:lo + fox_w].reshape(n_layers, d, N_FOX_HEADS, HEAD_DIM)
        return by_head(w, 2).reshape(n_layers, d, fox_w)

    w_gate_f = by_head(w_in[:, :, f_lo:f_hi], 2)
    g_hi = w_gate_f.astype(BF16)
    g_mid = (w_gate_f - g_hi.astype(F32)).astype(BF16)
    g_lo = ((w_gate_f - g_hi.astype(F32)) - g_mid.astype(F32)).astype(BF16)
    gate3 = jnp.pad(jnp.concatenate([g_hi, g_mid, g_lo], axis=2),
                    ((0, 0), (0, 0), (0, LANES - 3 * N_FOX_HEADS)))
    w_main = jnp.concatenate([bf(fox_cols(fx + fox_w)), bf(w_in[:, :, sb_w:2 * sb_w]),
                              bf(w_in[:, :, f_hi:])], axis=2)
    w_t = jnp.swapaxes(jnp.concatenate(
        [fox_cols(fx) * scale, fox_cols(fx + 2 * fox_w),
         w_in[:, :, :sb_w] * (scale * LOG2_E), w_in[:, :, 2 * sb_w:fx]], axis=2), 1, 2).astype(BF16)
    b_f = jnp.pad(by_head(b_forget, 1), ((0, 0), (0, LANES - N_FOX_HEADS)))[:, None, :]
    w_br_fox_sorted = by_head(w_br_fox.reshape(n_layers, N_FOX_HEADS, HEAD_DIM, d), 1).reshape(
        w_br_fox.shape)
    term, head = jnp.meshgrid(jnp.arange(3), jnp.arange(N_FOX_HEADS), indexing="ij")
    place = jnp.zeros((LANES, N_FOX_HEADS * LANES), BF16).at[
        term * N_FOX_HEADS + head, head * LANES + term].set(1.0)
    tok = jnp.arange(ROW_TILE)
    lower = (tok[:, None] >= tok[None, :]).astype(BF16)
    weights = dict(
        f1=(vec(ffn1_pre_g), vec(ffn1_post_g), bf(ffn1_w_gate), bf(ffn1_w_up), bf(ffn1_w_down)),
        f2=(vec(ffn2_pre_g), vec(ffn2_post_g), bf(ffn2_w_gate), bf(ffn2_w_up), bf(ffn2_w_down)),
        proj=(vec(mix_pre_g), w_main, w_t, gate3, b_f, lower, place),
        post=(vec(mix_pre_g), vec(mix_post_g), bf(w_gate), vec(b_gate), bf(w_br_sb),
              bf(w_br_fox_sorted), bf(w_br_mem), bf(w_out)),
    )

    t = ATT_TILE
    tri = (jnp.arange(t)[None, :] > jnp.arange(t)[:, None]).astype(BF16)

    kvmem = _memkv(mem.reshape(batch * mem.shape[1], d), mem_norm_g[None, :], bf(w_mem_kv))

    fox_k_w = (fox_w // LANES) * 3 * LANES
    k_sb_col = fox_k_w // LANES
    qmem_col = (fox_k_w + sb_w) // mem_w
    q_sb_row = 2 * fox_w // LANES
    v_sb_row = q_sb_row + sb_w // LANES
    nq = seq // t

    h = x.reshape(batch * seq, d)
    for layer in range(n_layers):
        l = jnp.full((1,), layer, jnp.int32)
        h = _ffn(l, h, *weights["f1"])
        rows, cols, c = _proj(l, h, *weights["proj"], batch)
        o_sb = _sb_attention(rows, cols, tri, batch, k_sb_col, q_sb_row, v_sb_row)
        c = c.reshape(batch, nq, t, LANES)
        c_ends = c[:, :, t - 1, :N_FOX_HEADS]
        k_max = jnp.sqrt(jnp.max(c[:, :, :, KNORM_LANE:KNORM_LANE + N_FOX_HEADS], axis=(1, 2)))
        c_ends = jnp.concatenate([c_ends, k_max[:, None, :]], axis=1)
        c_ends = c_ends.reshape(batch, nq + 1, N_FOX_HEADS // 2, 2).transpose(0, 2, 1, 3)
        o_fx = _fox_attention(rows, cols, c_ends, batch)
        h = _post(l, h, rows, o_sb, o_fx, kvmem, *weights["post"], batch, qmem_col)
        h = _ffn(l, h, *weights["f2"])
    return h.reshape(batch, seq, d)
```
